```python
import jax
import jax.numpy as jnp
from jax import lax
import numpy as np

D_MODEL = 1024
BATCH = 2
SEQ = 8192
DEPTH = 2

N_EVEN = (DEPTH + 1) // 2
N_ODD = DEPTH // 2
F32 = jnp.float32
NORM_EPS = 1e-6
D_FF = 2816
CONV_WIDTH = 4

LRU_WIDTH = D_MODEL // 2
LRU_BLOCKS = 8
LRU_BLOCK = LRU_WIDTH // LRU_BLOCKS
LRU_C = 8.0

RWKV_WIDTH = D_MODEL // 2
RWKV_HEADS = 8
RWKV_HEAD_DIM = RWKV_WIDTH // RWKV_HEADS
DECAY_RANK = 64
ICLR_RANK = 64
GATE_RANK = 128
RWKV_GN_EPS = 64e-5
SHIFT_WIDTH = 3 * RWKV_WIDTH + DECAY_RANK + ICLR_RANK + GATE_RANK
AB_IN = 2 * LRU_WIDTH + SHIFT_WIDTH
AB_MIX = LRU_WIDTH + RWKV_WIDTH

DN_HEADS = 8
DN_HEAD_DIM = 128
DN_WIDTH = DN_HEADS * DN_HEAD_DIM
DN_CHUNK = 64
C_IN = 4 * DN_WIDTH + 2 * DN_HEADS

kernel_name = 'hybrid_rglru_rwkv7_gdn_macaron'


def rmsnorm(x, w, eps=NORM_EPS):
    xf = x.astype(F32)
    y = xf * lax.rsqrt(jnp.mean(xf * xf, axis=-1, keepdims=True) + eps)
    return (y * w.astype(F32)).astype(x.dtype)


def swiglu(x, w_gate, w_up, w_down):
    return (jax.nn.silu(x @ w_gate) * (x @ w_up)) @ w_down


def causal_dwconv(x, w):
    k_w = w.shape[0]
    s = x.shape[1]
    xp = jnp.pad(x, ((0, 0), (k_w - 1, 0), (0, 0)))
    return sum(w[j] * xp[:, j:j + s] for j in range(k_w))


def token_shift(x):
    return jnp.pad(x[:, :-1], ((0, 0), (1, 0), (0, 0)))


def rg_lru(x, gate_a_w, gate_a_b, gate_x_w, gate_x_b, a_param):
    b, s, _ = x.shape
    xb = x.reshape(b, s, LRU_BLOCKS, LRU_BLOCK)
    r = jax.nn.sigmoid(jnp.einsum('bsgi,gij->bsgj', xb, gate_a_w).reshape(b, s, LRU_WIDTH) + gate_a_b)
    i = jax.nn.sigmoid(jnp.einsum('bsgi,gij->bsgj', xb, gate_x_w).reshape(b, s, LRU_WIDTH) + gate_x_b)
    log_a = -LRU_C * r.astype(F32) * jax.nn.softplus(-a_param.astype(F32))
    a = jnp.exp(log_a)
    u = jnp.sqrt(-jnp.expm1(2.0 * log_a)) * (i * x).astype(F32)

    def combine(c1, c2):
        a1, b1 = c1
        a2, b2 = c2
        return a1 * a2, a2 * b1 + b2

    _, h = lax.associative_scan(combine, (a, u), axis=1)
    return h.astype(x.dtype)


def rwkv7_time_mix(p, w0, w2, a0, a2, g2, k_k, k_a, r_k, ln_w, ln_b):
    b, s, _ = p.shape
    R = RWKV_WIDTH
    r = p[..., :R]
    k = p[..., R:2 * R]
    v = p[..., 2 * R:3 * R]
    o1 = 3 * R
    wl = p[..., o1:o1 + DECAY_RANK]
    al = p[..., o1 + DECAY_RANK:o1 + DECAY_RANK + ICLR_RANK]
    gl = p[..., o1 + DECAY_RANK + ICLR_RANK:]
    w_log = -jax.nn.softplus(-(w0 + jnp.tanh(wl) @ w2)) - 0.5
    decay = jnp.exp(-jnp.exp(w_log.astype(F32)))
    a = jax.nn.sigmoid(a0 + al @ a2)
    g = jax.nn.sigmoid(gl) @ g2

    def heads(t):
        return t.astype(F32).reshape(b, s, RWKV_HEADS, RWKV_HEAD_DIM)

    kk = heads(k * k_k)
    kk = kk / jnp.maximum(jnp.sqrt(jnp.sum(kk * kk, axis=-1, keepdims=True)), 1e-12)
    k_mod = k * (1.0 + (a - 1.0) * k_a)

    def time_major(t):
        return jnp.moveaxis(t, 1, 0)

    xs = (time_major(heads(r)), time_major(heads(decay)), time_major(heads(k_mod)),
          time_major(heads(v)), time_major(kk), time_major(heads(a)))

    def step(state, inp):
        r_t, w_t, k_t, v_t, kk_t, a_t = inp
        sa = jnp.einsum('bhvk,bhk->bhv', state, kk_t)
        state = (state * w_t[:, :, None, :]
                 - sa[..., None] * (kk_t * a_t)[:, :, None, :]
                 + v_t[..., None] * k_t[:, :, None, :])
        return state, jnp.einsum('bhvk,bhk->bhv', state, r_t)

    state0 = jnp.zeros((b, RWKV_HEADS, RWKV_HEAD_DIM, RWKV_HEAD_DIM), F32)
    _, y = lax.scan(step, state0, xs)
    y = jnp.moveaxis(y, 0, 1)
    mean = jnp.mean(y, axis=-1, keepdims=True)
    var = jnp.mean(jnp.square(y - mean), axis=-1, keepdims=True)
    yn = ((y - mean) * lax.rsqrt(var + RWKV_GN_EPS)).reshape(b, s, R) * ln_w + ln_b
    bonus = jnp.sum(heads(r) * heads(k_mod) * r_k.astype(F32), axis=-1, keepdims=True) * heads(v)
    return (yn + bonus.reshape(b, s, R)).astype(p.dtype) * g


def mixer_ab(h, w_in, conv_w, conv_b, ga_w, ga_b, gx_w, gx_b, a_param, mu,
             w0, w2, a0, a2, g2, k_k, k_a, r_k, ln_w, ln_b, w_out):
    z = h @ w_in
    L = LRU_WIDTH
    lru_x = z[..., :L]
    lru_gate = z[..., L:2 * L]
    shift_in = z[..., 2 * L:]
    xc = causal_dwconv(lru_x, conv_w) + conv_b
    y_a = rg_lru(xc, ga_w, ga_b, gx_w, gx_b, a_param) * jax.nn.gelu(lru_gate)
    p = shift_in + mu * (token_shift(shift_in) - shift_in)
    y_b = rwkv7_time_mix(p, w0, w2, a0, a2, g2, k_k, k_a, r_k, ln_w, ln_b)
    return jnp.concatenate([y_a, y_b], axis=-1) @ w_out


def gated_delta_rule_chunked(q, k, v, g, beta):
    b, s, nh, dk = q.shape
    dv = v.shape[-1]
    C = DN_CHUNK
    n = s // C

    def chunks(t):
        t = jnp.moveaxis(t.astype(F32), 2, 1)
        return t.reshape(t.shape[:2] + (n, C) + t.shape[3:])

    q, k, v, g, beta = chunks(q), chunks(k), chunks(v), chunks(g), chunks(beta)
    gc = jnp.cumsum(g, axis=-1)
    causal = jnp.tril(jnp.ones((C, C), bool))
    strict = jnp.tril(jnp.ones((C, C), bool), -1)
    decay = jnp.exp(jnp.where(causal, gc[..., :, None] - gc[..., None, :], -jnp.inf))
    kk = jnp.einsum('bhnid,bhnjd->bhnij', k, k)
    A = jnp.where(strict, beta[..., :, None] * kk * decay, 0.0)
    eye = jnp.eye(C, dtype=F32)
    rhs = jnp.concatenate([v * beta[..., None], k * (beta * jnp.exp(gc))[..., None]], axis=-1)
    sol = lax.linalg.triangular_solve(A + eye, rhs, left_side=True, lower=True, unit_diagonal=True)
    u = sol[..., :dv]
    w = sol[..., dv:]
    attn = jnp.einsum('bhnid,bhnjd->bhnij', q, k) * decay
    q_dec = q * jnp.exp(gc)[..., None]
    k_dec = k * jnp.exp(gc[..., -1:] - gc)[..., None]
    chunk_decay = jnp.exp(gc[..., -1])
    xs = tuple(jnp.moveaxis(t, 2, 0) for t in (u, w, q_dec, k_dec, attn, chunk_decay))

    def step(state, inp):
        u_c, w_c, qd_c, kd_c, at_c, cd_c = inp
        v_new = u_c - jnp.einsum('bhck,bhkv->bhcv', w_c, state)
        o = jnp.einsum('bhck,bhkv->bhcv', qd_c, state) + jnp.einsum('bhij,bhjv->bhiv', at_c, v_new)
        state = state * cd_c[..., None, None] + jnp.einsum('bhck,bhcv->bhkv', kd_c, v_new)
        return state, o

    _, o = lax.scan(step, jnp.zeros((b, nh, dk, dv), F32), xs)
    o = jnp.moveaxis(o, 0, 2).reshape(b, nh, s, dv)
    return jnp.moveaxis(o, 1, 2)


def l2norm(t, eps=1e-6):
    tf = t.astype(F32)
    return tf * lax.rsqrt(jnp.sum(tf * tf, axis=-1, keepdims=True) + eps)


def mixer_c(h, w_in, conv_w, A_log, dt_bias, norm_w, w_out):
    b, s, _ = h.shape
    W = DN_WIDTH
    z = h @ w_in
    qkv = jax.nn.silu(causal_dwconv(z[..., :3 * W], conv_w))
    gate = z[..., 3 * W:4 * W].reshape(b, s, DN_HEADS, DN_HEAD_DIM)
    beta = jax.nn.sigmoid(z[..., 4 * W:4 * W + DN_HEADS].astype(F32))
    alpha = z[..., 4 * W + DN_HEADS:].astype(F32)
    g = -jnp.exp(A_log.astype(F32)) * jax.nn.softplus(alpha + dt_bias.astype(F32))
    q = l2norm(qkv[..., :W].reshape(b, s, DN_HEADS, DN_HEAD_DIM)) * (DN_HEAD_DIM ** -0.5)
    k = l2norm(qkv[..., W:2 * W].reshape(b, s, DN_HEADS, DN_HEAD_DIM))
    v = qkv[..., 2 * W:].reshape(b, s, DN_HEADS, DN_HEAD_DIM)
    o = gated_delta_rule_chunked(q, k, v, g, beta)
    o = rmsnorm(o, norm_w) * jax.nn.silu(gate.astype(F32))
    return o.reshape(b, s, W).astype(h.dtype) @ w_out


def setup_inputs(seed: int = 0) -> dict:
    key = jax.random.key(seed)
    ks = iter(jax.random.split(key, 48))

    def nrm(shape, scale):
        return jax.random.normal(next(ks), shape, F32) * scale

    def unif(shape, lo, hi):
        return jax.random.uniform(next(ks), shape, F32, lo, hi)

    E, O, D = N_EVEN, N_ODD, D_MODEL
    a_pow = unif((E, LRU_WIDTH), 0.9, 0.999)
    a_base = a_pow ** (1.0 / LRU_C)
    dt = jnp.exp(unif((O, DN_HEADS), float(np.log(1e-3)), float(np.log(1e-1))))
    return {
        'x': nrm((BATCH, SEQ, D), 1.0),
        'ffn_norm': 1.0 + nrm((DEPTH, 2, D), 0.02),
        'ffn_w_gate': nrm((DEPTH, 2, D, D_FF), D ** -0.5),
        'ffn_w_up': nrm((DEPTH, 2, D, D_FF), D ** -0.5),
        'ffn_w_down': nrm((DEPTH, 2, D_FF, D), D_FF ** -0.5),
        'mix_norm': 1.0 + nrm((DEPTH, D), 0.02),
        'final_norm': 1.0 + nrm((D,), 0.02),
        'ab_w_in': nrm((E, D, AB_IN), D ** -0.5),
        'lru_conv_w': nrm((E, CONV_WIDTH, LRU_WIDTH), CONV_WIDTH ** -0.5),
        'lru_conv_b': nrm((E, LRU_WIDTH), 0.02),
        'lru_gate_a_w': nrm((E, LRU_BLOCKS, LRU_BLOCK, LRU_BLOCK), LRU_BLOCK ** -0.5),
        'lru_gate_a_b': nrm((E, LRU_WIDTH), 0.02),
        'lru_gate_x_w': nrm((E, LRU_BLOCKS, LRU_BLOCK, LRU_BLOCK), LRU_BLOCK ** -0.5),
        'lru_gate_x_b': nrm((E, LRU_WIDTH), 0.02),
        'lru_a_param': jnp.log(a_base) - jnp.log1p(-a_base),
        'rwkv_mu': unif((E, SHIFT_WIDTH), 0.0, 1.0),
        'rwkv_w0': unif((E, RWKV_WIDTH), -6.0, -1.0),
        'rwkv_w2': nrm((E, DECAY_RANK, RWKV_WIDTH), 0.1 * DECAY_RANK ** -0.5),
        'rwkv_a0': nrm((E, RWKV_WIDTH), 0.1),
        'rwkv_a2': nrm((E, ICLR_RANK, RWKV_WIDTH), ICLR_RANK ** -0.5),
        'rwkv_g2': nrm((E, GATE_RANK, RWKV_WIDTH), GATE_RANK ** -0.5),
        'rwkv_k_k': 0.85 + nrm((E, RWKV_WIDTH), 0.02),
        'rwkv_k_a': 1.0 + nrm((E, RWKV_WIDTH), 0.02),
        'rwkv_r_k': nrm((E, RWKV_HEADS, RWKV_HEAD_DIM), 0.1),
        'rwkv_ln_w': 1.0 + nrm((E, RWKV_WIDTH), 0.02),
        'rwkv_ln_b': nrm((E, RWKV_WIDTH), 0.02),
        'ab_w_out': nrm((E, AB_MIX, D), AB_MIX ** -0.5),
        'c_w_in': nrm((O, D, C_IN), D ** -0.5),
        'dn_conv_w': nrm((O, CONV_WIDTH, 3 * DN_WIDTH), CONV_WIDTH ** -0.5),
        'dn_A_log': jnp.log(unif((O, DN_HEADS), 1.0, 16.0)),
        'dn_dt_bias': dt + jnp.log(-jnp.expm1(-dt)),
        'dn_norm': 1.0 + nrm((O, DN_HEAD_DIM), 0.02),
        'c_w_out': nrm((O, DN_WIDTH, D), DN_WIDTH ** -0.5),
    }


def reference(x, ffn_norm, ffn_w_gate, ffn_w_up, ffn_w_down, mix_norm, final_norm,
              ab_w_in, lru_conv_w, lru_conv_b, lru_gate_a_w, lru_gate_a_b, lru_gate_x_w,
              lru_gate_x_b, lru_a_param, rwkv_mu, rwkv_w0, rwkv_w2, rwkv_a0, rwkv_a2,
              rwkv_g2, rwkv_k_k, rwkv_k_a, rwkv_r_k, rwkv_ln_w, rwkv_ln_b, ab_w_out,
              c_w_in, dn_conv_w, dn_A_log, dn_dt_bias, dn_norm, c_w_out):
    for l in range(DEPTH):
        i = l // 2
        x = x + 0.5 * swiglu(rmsnorm(x, ffn_norm[l, 0]), ffn_w_gate[l, 0], ffn_w_up[l, 0], ffn_w_down[l, 0])
        h = rmsnorm(x, mix_norm[l])
        if l % 2 == 0:
            x = x + mixer_ab(h, ab_w_in[i], lru_conv_w[i], lru_conv_b[i], lru_gate_a_w[i],
                             lru_gate_a_b[i], lru_gate_x_w[i], lru_gate_x_b[i], lru_a_param[i],
                             rwkv_mu[i], rwkv_w0[i], rwkv_w2[i], rwkv_a0[i], rwkv_a2[i],
                             rwkv_g2[i], rwkv_k_k[i], rwkv_k_a[i], rwkv_r_k[i], rwkv_ln_w[i],
                             rwkv_ln_b[i], ab_w_out[i])
        else:
            x = x + mixer_c(h, c_w_in[i], dn_conv_w[i], dn_A_log[i], dn_dt_bias[i], dn_norm[i], c_w_out[i])
        x = x + 0.5 * swiglu(rmsnorm(x, ffn_norm[l, 1]), ffn_w_gate[l, 1], ffn_w_up[l, 1], ffn_w_down[l, 1])
    return rmsnorm(x, final_norm)
```

```python
import functools

import jax
import jax.numpy as jnp
from jax import lax
from jax.experimental import pallas as pl
from jax.experimental.pallas import tpu as pltpu

F32 = jnp.float32
BF16 = jnp.bfloat16
HIGHEST = lax.Precision.HIGHEST

D_MODEL = 1024
D_FF = 2816
NORM_EPS = 1e-6
CONV_WIDTH = 4

LRU_WIDTH = 512
LRU_BLOCK = 64
LRU_C = 8.0

RWKV_WIDTH = 512
RWKV_HEAD_DIM = 64
DECAY_RANK = 64
ICLR_RANK = 64
GATE_RANK = 128
RWKV_GN_EPS = 64e-5
SHIFT_WIDTH = 3 * RWKV_WIDTH + DECAY_RANK + ICLR_RANK + GATE_RANK
AB_IN = 2 * LRU_WIDTH + SHIFT_WIDTH

DN_HEADS = 8
DN_HEAD_DIM = 128
DN_WIDTH = DN_HEADS * DN_HEAD_DIM
C_SMALL = 128
C_IN_PAD = 4 * DN_WIDTH + C_SMALL

LANES = 128
CHUNK = 64
PAIR = LANES // RWKV_HEAD_DIM
N_PAIRS = RWKV_WIDTH // LANES
INV_LEVELS = 6
VMEM_LIMIT = 56 * 1024 * 1024


def _dot(a, b):
    return jnp.dot(a, b, preferred_element_type=F32)


def _dot_hi(a, b):
    return lax.dot_general(a, b, (((1,), (0,)), ((), ())), precision=HIGHEST,
                           preferred_element_type=F32)


def _dot_nt(a, b):
    return lax.dot_general(a, b, (((1,), (1,)), ((), ())), precision=HIGHEST,
                           preferred_element_type=F32)


def _dot_tn(a, b):
    return lax.dot_general(a, b, (((0,), (0,)), ((), ())), precision=HIGHEST,
                           preferred_element_type=F32)


def _rmsnorm(x, w, eps=NORM_EPS):
    return x * lax.rsqrt(jnp.mean(x * x, axis=-1, keepdims=True) + eps) * w


def _sigmoid(x):
    return 1.0 / (1.0 + jnp.exp(-x))


def _silu(x):
    return x * _sigmoid(x)


def _softplus(x):
    return jnp.maximum(x, 0.0) + jnp.log(1.0 + jnp.exp(-jnp.abs(x)))


def _gelu_tanh(x):
    return 0.5 * x * (1.0 + jnp.tanh(0.7978845608028654 * (x + 0.044715 * x * x * x)))


def _iota(shape, dim):
    return lax.broadcasted_iota(jnp.int32, shape, dim)


def _shift_rows(x, x_prev, k, row):
    return jnp.where(row >= k, pltpu.roll(x, k, 0), pltpu.roll(x_prev, k, 0))


def _causal_conv(x, x_prev, w_ref, row):
    y = w_ref[CONV_WIDTH - 1:CONV_WIDTH, :] * x
    for k in range(1, CONV_WIDTH):
        y = y + w_ref[CONV_WIDTH - 1 - k:CONV_WIDTH - k, :] * _shift_rows(x, x_prev, k, row)
    return y


def _unit_lower_inverse(a):
    inv = jnp.where(_iota(a.shape, 0) == _iota(a.shape, 1), 1.0, 0.0) - a
    p = a
    for _ in range(INV_LEVELS - 1):
        p = _dot_hi(p, p)
        inv = inv + _dot_hi(inv, p)
    return inv


def _linear_scan(a, b, row):
    k = 1
    while k < a.shape[0]:
        a_s = jnp.where(row >= k, pltpu.roll(a, k, 0), 1.0)
        b_s = jnp.where(row >= k, pltpu.roll(b, k, 0), 0.0)
        b = a * b_s + b
        a = a * a_s
        k *= 2
    return a, b


def _ffn_kernel(x_ref, nw_ref, wg_ref, wu_ref, wd_ref, fw_ref, o_ref, *, f_chunk, final_norm):
    x = x_ref[...]
    h = _rmsnorm(x, nw_ref[...]).astype(BF16)
    acc = jnp.zeros_like(x)
    for j in range(D_FF // f_chunk):
        lo = j * f_chunk
        g = _dot(h, wg_ref[:, lo:lo + f_chunk])
        u = _dot(h, wu_ref[:, lo:lo + f_chunk])
        acc = acc + _dot((_silu(g) * u).astype(BF16), wd_ref[lo:lo + f_chunk, :])
    y = x + 0.5 * acc
    if final_norm:
        y = _rmsnorm(y, fw_ref[...])
    o_ref[...] = y


def _ffn(x, norm_w, w_gate, w_up, w_down, final_w, *, final_norm, tm=512, f_chunk=256):
    t, d = x.shape
    const = lambda i: (0, 0)
    return pl.pallas_call(
        functools.partial(_ffn_kernel, f_chunk=f_chunk, final_norm=final_norm),
        grid=(t // tm,),
        in_specs=[
            pl.BlockSpec((tm, d), lambda i: (i, 0)),
            pl.BlockSpec((1, d), const),
            pl.BlockSpec((d, D_FF), const),
            pl.BlockSpec((d, D_FF), const),
            pl.BlockSpec((D_FF, d), const),
            pl.BlockSpec((1, d), const),
        ],
        out_specs=pl.BlockSpec((tm, d), lambda i: (i, 0)),
        out_shape=jax.ShapeDtypeStruct((t, d), F32),
        compiler_params=pltpu.CompilerParams(
            dimension_semantics=("arbitrary",), vmem_limit_bytes=VMEM_LIMIT),
        name="ffn",
    )(x, norm_w.reshape(1, d), w_gate.astype(BF16), w_up.astype(BF16), w_down.astype(BF16),
      final_w.reshape(1, d))


def _norm_proj_kernel(x_ref, nw_ref, w_ref, o_ref):
    h = _rmsnorm(x_ref[...], nw_ref[...]).astype(BF16)
    o_ref[...] = _dot(h, w_ref[...])


def _norm_proj(x, norm_w, w, *, tm=512):
    t, d = x.shape
    n = w.shape[1]
    return pl.pallas_call(
        _norm_proj_kernel,
        grid=(t // tm,),
        in_specs=[
            pl.BlockSpec((tm, d), lambda i: (i, 0)),
            pl.BlockSpec((1, d), lambda i: (0, 0)),
            pl.BlockSpec((d, n), lambda i: (0, 0)),
        ],
        out_specs=pl.BlockSpec((tm, n), lambda i: (i, 0)),
        out_shape=jax.ShapeDtypeStruct((t, n), F32),
        compiler_params=pltpu.CompilerParams(
            dimension_semantics=("arbitrary",), vmem_limit_bytes=VMEM_LIMIT),
        name="norm_proj",
    )(x, norm_w.reshape(1, d), w.astype(BF16))


def _out_proj_kernel(x_ref, y_ref, w_ref, o_ref):
    o_ref[...] = x_ref[...] + _dot(y_ref[...].astype(BF16), w_ref[...])


def _out_proj(x, y, w, *, tm=512):
    t, d = x.shape
    k = y.shape[1]
    return pl.pallas_call(
        _out_proj_kernel,
        grid=(t // tm,),
        in_specs=[
            pl.BlockSpec((tm, d), lambda i: (i, 0)),
            pl.BlockSpec((tm, k), lambda i: (i, 0)),
            pl.BlockSpec((k, d), lambda i: (0, 0)),
        ],
        out_specs=pl.BlockSpec((tm, d), lambda i: (i, 0)),
        out_shape=jax.ShapeDtypeStruct((t, d), F32),
        compiler_params=pltpu.CompilerParams(
            dimension_semantics=("arbitrary",), vmem_limit_bytes=VMEM_LIMIT),
        name="out_proj",
    )(x, y, w.astype(BF16))


def _stack(x, m0, m1):
    return jnp.concatenate([x * m0, x * m1], axis=0)


def _unstack(y):
    return y[:CHUNK] + y[CHUNK:]


def _mixer_ab_kernel(z_ref, cw_ref, cb_ref, wa_ref, ba_ref, wx_ref, bx_ref, ap_ref, mu_ref,
                     w0_ref, w2_ref, a0_ref, a2_ref, g2_ref, kk_ref, ka_ref, rk_ref,
                     lnw_ref, lnb_ref, o_ref, zprev_ref, hlru_ref, state_ref):
    n_batch = z_ref.shape[0]
    c = CHUNK
    L = LRU_WIDTH

    @pl.when(pl.program_id(0) == 0)
    def _():
        zprev_ref[...] = jnp.zeros_like(zprev_ref)
        hlru_ref[...] = jnp.zeros_like(hlru_ref)
        state_ref[...] = jnp.zeros_like(state_ref)

    row = _iota((c, 1), 0)
    lane = _iota((1, LANES), 1)
    m0 = jnp.where(lane < RWKV_HEAD_DIM, 1.0, 0.0)
    m1 = 1.0 - m0
    ri = _iota((c, c), 0)
    ci = _iota((c, c), 1)
    l_incl = jnp.where(ri >= ci, 1.0, 0.0)
    si = _iota((LANES, LANES), 0)
    sj = _iota((LANES, LANES), 1)
    same_head = (si // RWKV_HEAD_DIM) == (sj // RWKV_HEAD_DIM)
    ones_bd = jnp.where(same_head, 1.0, 0.0)
    strict_bd = same_head & (si > sj)
    incl_bd = same_head & (si >= sj)
    inv_n = 1.0 / RWKV_HEAD_DIM

    for b in range(n_batch):
        z = z_ref[b]
        zp = zprev_ref[b]

        lx = z[:, :L]
        xc = _causal_conv(lx, zp[:, :L], cw_ref, row) + cb_ref[...]
        xcb = xc.astype(BF16)
        sp_a = _softplus(-ap_ref[...])
        for p in range(N_PAIRS):
            sl = slice(p * LANES, (p + 1) * LANES)
            xcp = xc[:, sl]
            r_gate = _sigmoid(_dot(xcb[:, sl], wa_ref[p]) + ba_ref[:, sl])
            i_gate = _sigmoid(_dot(xcb[:, sl], wx_ref[p]) + bx_ref[:, sl])
            log_a = -LRU_C * r_gate * sp_a[:, sl]
            a = jnp.exp(log_a)
            u = jnp.sqrt(jnp.tanh(-log_a) * (a * a + 1.0)) * (i_gate * xcp)
            a_cum, h = _linear_scan(a, u, row)
            h = h + a_cum * hlru_ref[b, 0:1, sl]
            hlru_ref[b, 0:1, sl] = h[c - 1:c, :]
            o_ref[b, :, sl] = h * _gelu_tanh(z[:, L + p * LANES:L + (p + 1) * LANES])

        s = z[:, 2 * L:]
        pmix = s + mu_ref[...] * (_shift_rows(s, zp[:, 2 * L:], 1, row) - s)
        R = RWKV_WIDTH
        r = pmix[:, :R]
        k = pmix[:, R:2 * R]
        v = pmix[:, 2 * R:3 * R]
        wa_in = pmix[:, 3 * R:3 * R + LANES]
        gl_in = pmix[:, 3 * R + LANES:]
        w_log = -_softplus(-(w0_ref[...] + _dot(jnp.tanh(wa_in).astype(BF16), w2_ref[...]))) - 0.5
        lw = -jnp.exp(w_log)
        a_lr = _sigmoid(a0_ref[...] + _dot(wa_in.astype(BF16), a2_ref[...]))
        g_out = _dot(_sigmoid(gl_in).astype(BF16), g2_ref[...])
        kkr = k * kk_ref[...]
        k_mod = k * (1.0 + (a_lr - 1.0) * ka_ref[...])
        cs = _dot_hi(l_incl, lw)
        cl = cs[c - 1:c, :]
        g_in = jnp.exp(cs)
        g_ex = jnp.exp(cs - lw)
        g_inv = jnp.exp(-cs)
        g_end = jnp.exp(cl - cs)
        g_tot = jnp.exp(cl)

        for p in range(N_PAIRS):
            sl = slice(p * LANES, (p + 1) * LANES)
            rp, vp, ktp = r[:, sl], v[:, sl], k_mod[:, sl]
            kn = kkr[:, sl]
            n2 = _dot_hi(kn * kn, ones_bd)
            kkp = kn / jnp.maximum(jnp.sqrt(n2), 1e-12)
            bp = kkp * a_lr[:, sl]
            kg_st = _stack(kkp * g_ex[:, sl], m0, m1)
            rg = rp * g_in[:, sl]
            bi_st = _stack(bp * g_inv[:, sl], m0, m1)
            ki_st = _stack(ktp * g_inv[:, sl], m0, m1)
            kd_st = _stack(ktp * g_end[:, sl], m0, m1)
            bd_st = _stack(bp * g_end[:, sl], m0, m1)
            v_st = _stack(vp, m0, m1)
            lhs = jnp.concatenate([kg_st, _stack(rg, m0, m1)], axis=0)
            rhs = jnp.concatenate([bi_st, ki_st], axis=0)
            m = _dot_nt(lhs, rhs)
            a_m = jnp.where(strict_bd, m[:2 * c, :2 * c], 0.0)
            b_m = jnp.where(strict_bd, m[:2 * c, 2 * c:], 0.0)
            q_m = jnp.where(incl_bd, m[2 * c:, :2 * c], 0.0)
            p_m = jnp.where(incl_bd, m[2 * c:, 2 * c:], 0.0)
            t_inv = _unit_lower_inverse(a_m)
            bv = _dot_hi(b_m, v_st)
            wu = _dot_hi(t_inv, jnp.concatenate([kg_st, bv], axis=1))
            st = state_ref[b, p]
            u_st = _dot_nt(wu[:, :LANES], st) + wu[:, LANES:]
            y = _dot_nt(rg, st) + _unstack(_dot_hi(p_m, v_st) - _dot_hi(q_m, u_st))
            state_ref[b, p] = st * g_tot[:, sl] + _dot_tn(v_st, kd_st) - _dot_tn(u_st, bd_st)

            mean = _dot_hi(y, ones_bd) * inv_n
            d = y - mean
            var = _dot_hi(d * d, ones_bd) * inv_n
            yn = d * lax.rsqrt(var + RWKV_GN_EPS) * lnw_ref[:, sl] + lnb_ref[:, sl]
            bonus = _dot_hi(rp * ktp * rk_ref[:, sl], ones_bd) * vp
            o_ref[b, :, L + p * LANES:L + (p + 1) * LANES] = (yn + bonus) * g_out[:, sl]

        zprev_ref[b] = z


def _pair_blockdiag(w):
    g = w.reshape(N_PAIRS, PAIR, LRU_BLOCK, LRU_BLOCK)
    z = jnp.zeros((N_PAIRS, LRU_BLOCK, LRU_BLOCK), w.dtype)
    top = jnp.concatenate([g[:, 0], z], axis=2)
    bot = jnp.concatenate([z, g[:, 1]], axis=2)
    return jnp.concatenate([top, bot], axis=1)


def _mixer_ab(z, conv_w, conv_b, ga_w, ga_b, gx_w, gx_b, a_param, mu, w0, w2, a0, a2, g2,
              k_k, k_a, r_k, ln_w, ln_b):
    b, s, n = z.shape
    row = lambda t: t.reshape(1, -1)
    w2p = jnp.concatenate([w2, jnp.zeros((ICLR_RANK, RWKV_WIDTH), F32)], axis=0).astype(BF16)
    a2p = jnp.concatenate([jnp.zeros((DECAY_RANK, RWKV_WIDTH), F32), a2], axis=0).astype(BF16)
    args = (z, conv_w, row(conv_b), _pair_blockdiag(ga_w).astype(BF16), row(ga_b),
            _pair_blockdiag(gx_w).astype(BF16), row(gx_b), row(a_param), row(mu), row(w0), w2p,
            row(a0), a2p, g2.astype(BF16), row(k_k), row(k_a), row(r_k), row(ln_w), row(ln_b))

    def full(a):
        nd = a.ndim
        return pl.BlockSpec(a.shape, lambda i: (0,) * nd)

    return pl.pallas_call(
        _mixer_ab_kernel,
        grid=(s // CHUNK,),
        in_specs=[pl.BlockSpec((b, CHUNK, n), lambda i: (0, i, 0))] + [full(a) for a in args[1:]],
        out_specs=pl.BlockSpec((b, CHUNK, D_MODEL), lambda i: (0, i, 0)),
        out_shape=jax.ShapeDtypeStruct((b, s, D_MODEL), F32),
        scratch_shapes=[
            pltpu.VMEM((b, CHUNK, n), F32),
            pltpu.VMEM((b, 8, LRU_WIDTH), F32),
            pltpu.VMEM((b, N_PAIRS, LANES, LANES), F32),
        ],
        compiler_params=pltpu.CompilerParams(
            dimension_semantics=("arbitrary",), vmem_limit_bytes=VMEM_LIMIT),
        name="mixer_ab",
    )(*args)


def _mixer_c_kernel(z_ref, cw_ref, alog_ref, dt_ref, nw_ref, o_ref, zprev_ref, state_ref):
    n_batch = z_ref.shape[0]
    c = CHUNK
    W = DN_WIDTH
    dh = DN_HEAD_DIM

    @pl.when(pl.program_id(0) == 0)
    def _():
        zprev_ref[...] = jnp.zeros_like(zprev_ref)
        state_ref[...] = jnp.zeros_like(state_ref)

    row = _iota((c, 1), 0)
    ri = _iota((c, c), 0)
    ci = _iota((c, c), 1)
    causal = ri >= ci
    strict = ri > ci
    l_incl = jnp.where(causal, 1.0, 0.0)

    for b in range(n_batch):
        zq = z_ref[b, :, :3 * W]
        qkv = _silu(_causal_conv(zq, zprev_ref[b], cw_ref, row))
        zs = z_ref[b, :, 4 * W:]
        beta_all = _sigmoid(zs)
        g_all = -jnp.exp(alog_ref[...]) * _softplus(zs + dt_ref[...])
        gc_all = _dot_hi(l_incl, g_all)
        gc_rows = gc_all.T
        for h in range(DN_HEADS):
            sl = slice(h * dh, (h + 1) * dh)
            beta = beta_all[:, h:h + 1]
            gcol = gc_all[:, DN_HEADS + h:DN_HEADS + h + 1]
            grow = gc_rows[DN_HEADS + h:DN_HEADS + h + 1, :]
            glast = gcol[c - 1:c, :]
            qh = qkv[:, sl]
            kh = qkv[:, W + h * dh:W + (h + 1) * dh]
            vh = qkv[:, 2 * W + h * dh:2 * W + (h + 1) * dh]
            qh = qh * lax.rsqrt(jnp.sum(qh * qh, axis=-1, keepdims=True) + 1e-6) * (dh ** -0.5)
            kh = kh * lax.rsqrt(jnp.sum(kh * kh, axis=-1, keepdims=True) + 1e-6)
            decay = jnp.exp(jnp.where(causal, gcol - grow, -jnp.inf))
            a_m = jnp.where(strict, beta * _dot_nt(kh, kh) * decay, 0.0)
            t_inv = _unit_lower_inverse(a_m)
            egc = jnp.exp(gcol)
            sol = _dot_hi(t_inv, jnp.concatenate([vh * beta, kh * (beta * egc)], axis=1))
            u = sol[:, :dh]
            w = sol[:, dh:]
            attn = _dot_nt(qh, kh) * decay
            st = state_ref[b, h]
            v_new = u - _dot_hi(w, st)
            o = _dot_hi(qh * egc, st) + _dot_hi(attn, v_new)
            state_ref[b, h] = st * jnp.exp(glast) + _dot_tn(kh * jnp.exp(glast - gcol), v_new)
            gate = z_ref[b, :, 3 * W + h * dh:3 * W + (h + 1) * dh]
            o_ref[b, :, sl] = _rmsnorm(o, nw_ref[...]) * _silu(gate)
        zprev_ref[b] = zq


def _mixer_c(z, conv_w, a_log, dt_bias, norm_w):
    b, s, n = z.shape
    pad = jnp.zeros((C_SMALL - 2 * DN_HEADS,), F32)
    lead = jnp.zeros((DN_HEADS,), F32)
    alog_row = jnp.concatenate([lead, a_log, pad]).reshape(1, C_SMALL)
    dt_row = jnp.concatenate([lead, dt_bias, pad]).reshape(1, C_SMALL)
    args = (z, conv_w, alog_row, dt_row, norm_w.reshape(1, DN_HEAD_DIM))

    def full(a):
        nd = a.ndim
        return pl.BlockSpec(a.shape, lambda i: (0,) * nd)

    return pl.pallas_call(
        _mixer_c_kernel,
        grid=(s // CHUNK,),
        in_specs=[pl.BlockSpec((b, CHUNK, n), lambda i: (0, i, 0))] + [full(a) for a in args[1:]],
        out_specs=pl.BlockSpec((b, CHUNK, DN_WIDTH), lambda i: (0, i, 0)),
        out_shape=jax.ShapeDtypeStruct((b, s, DN_WIDTH), F32),
        scratch_shapes=[
            pltpu.VMEM((b, CHUNK, 3 * DN_WIDTH), F32),
            pltpu.VMEM((b, DN_HEADS, DN_HEAD_DIM, DN_HEAD_DIM), F32),
        ],
        compiler_params=pltpu.CompilerParams(
            dimension_semantics=("arbitrary",), vmem_limit_bytes=VMEM_LIMIT),
        name="mixer_c",
    )(*args)


def kernel(x, ffn_norm, ffn_w_gate, ffn_w_up, ffn_w_down, mix_norm, final_norm, ab_w_in, lru_conv_w, lru_conv_b, lru_gate_a_w, lru_gate_a_b, lru_gate_x_w, lru_gate_x_b, lru_a_param, rwkv_mu, rwkv_w0, rwkv_w2, rwkv_a0, rwkv_a2, rwkv_g2, rwkv_k_k, rwkv_k_a, rwkv_r_k, rwkv_ln_w, rwkv_ln_b, ab_w_out, c_w_in, dn_conv_w, dn_A_log, dn_dt_bias, dn_norm, c_w_out):
    b, s, d = x.shape
    depth = ffn_norm.shape[0]
    xf = x.reshape(b * s, d)
    for l in range(depth):
        i = l // 2
        xf = _ffn(xf, ffn_norm[l, 0], ffn_w_gate[l, 0], ffn_w_up[l, 0], ffn_w_down[l, 0],
                  final_norm, final_norm=False)
        if l % 2 == 0:
            z = _norm_proj(xf, mix_norm[l], ab_w_in[i]).reshape(b, s, AB_IN)
            y = _mixer_ab(z, lru_conv_w[i], lru_conv_b[i], lru_gate_a_w[i], lru_gate_a_b[i],
                          lru_gate_x_w[i], lru_gate_x_b[i], lru_a_param[i], rwkv_mu[i],
                          rwkv_w0[i], rwkv_w2[i], rwkv_a0[i], rwkv_a2[i], rwkv_g2[i],
                          rwkv_k_k[i], rwkv_k_a[i], rwkv_r_k[i], rwkv_ln_w[i], rwkv_ln_b[i])
            xf = _out_proj(xf, y.reshape(b * s, D_MODEL), ab_w_out[i])
        else:
            w_in = jnp.concatenate(
                [c_w_in[i], jnp.zeros((d, C_IN_PAD - c_w_in.shape[2]), F32)], axis=1)
            z = _norm_proj(xf, mix_norm[l], w_in).reshape(b, s, C_IN_PAD)
            y = _mixer_c(z, dn_conv_w[i], dn_A_log[i], dn_dt_bias[i], dn_norm[i])
            xf = _out_proj(xf, y.reshape(b * s, DN_WIDTH), c_w_out[i])
        xf = _ffn(xf, ffn_norm[l, 1], ffn_w_gate[l, 1], ffn_w_up[l, 1], ffn_w_down[l, 1],
                  final_norm, final_norm=(l == depth - 1))
    return xf.reshape(b, s, d)
```

```python
import functools

import jax
import jax.numpy as jnp
from jax import lax
from jax.experimental import pallas as pl
from jax.experimental.pallas import tpu as pltpu

F32 = jnp.float32
BF16 = jnp.bfloat16
HIGHEST = lax.Precision.HIGHEST

D_MODEL = 1024
D_FF = 2816
NORM_EPS = 1e-6
CONV_WIDTH = 4

LRU_WIDTH = 512
LRU_BLOCK = 64
LRU_C = 8.0

RWKV_WIDTH = 512
RWKV_HEAD_DIM = 64
DECAY_RANK = 64
ICLR_RANK = 64
GATE_RANK = 128
RWKV_GN_EPS = 64e-5
SHIFT_WIDTH = 3 * RWKV_WIDTH + DECAY_RANK + ICLR_RANK + GATE_RANK
AB_IN = 2 * LRU_WIDTH + SHIFT_WIDTH

DN_HEADS = 8
DN_HEAD_DIM = 128
DN_WIDTH = DN_HEADS * DN_HEAD_DIM
C_SMALL = 128
C_IN_PAD = 4 * DN_WIDTH + C_SMALL

LANES = 128
CHUNK = 64
PAIR = LANES // RWKV_HEAD_DIM
N_PAIRS = RWKV_WIDTH // LANES
INV_LEVELS = 6
VMEM_LIMIT = 56 * 1024 * 1024


def _dot(a, b):
    return jnp.dot(a, b, preferred_element_type=F32)


REC_PASSES = 1

_NN = (((1,), (0,)), ((), ()))
_NT = (((1,), (1,)), ((), ()))
_TN = (((0,), (0,)), ((), ()))


def _split(x, parts):
    out = []
    for _ in range(parts - 1):
        hi = x.astype(BF16)
        out.append(hi)
        x = x - hi.astype(F32)
    out.append(x.astype(BF16))
    return out


def _rec_dot(a, b, dims):
    if REC_PASSES == 6:
        return lax.dot_general(a, b, dims, precision=HIGHEST, preferred_element_type=F32)
    dg = lambda x, y: lax.dot_general(x, y, dims, preferred_element_type=F32)
    if REC_PASSES == 1:
        return dg(a.astype(BF16), b.astype(BF16))
    a_hi, a_lo = _split(a, 2)
    b_hi, b_lo = _split(b, 2)
    return dg(a_hi, b_hi) + dg(a_hi, b_lo) + dg(a_lo, b_hi)


def _dot_hi(a, b):
    return _rec_dot(a, b, _NN)


def _dot_nt(a, b):
    return _rec_dot(a, b, _NT)


def _dot_tn(a, b):
    return _rec_dot(a, b, _TN)


def _dot_01_lhs(l, x):
    lb = l.astype(BF16)
    return sum(jnp.dot(lb, xp, preferred_element_type=F32) for xp in _split(x, 3))


def _dot_01_rhs(x, r):
    rb = r.astype(BF16)
    return sum(jnp.dot(xp, rb, preferred_element_type=F32) for xp in _split(x, 3))


def _rmsnorm(x, w, eps=NORM_EPS):
    return x * lax.rsqrt(jnp.mean(x * x, axis=-1, keepdims=True) + eps) * w


def _sigmoid(x):
    return 1.0 / (1.0 + jnp.exp(-x))


def _silu(x):
    return x * _sigmoid(x)


def _softplus(x):
    return jnp.maximum(x, 0.0) + jnp.log(1.0 + jnp.exp(-jnp.abs(x)))


def _gelu_tanh(x):
    return 0.5 * x * (1.0 + jnp.tanh(0.7978845608028654 * (x + 0.044715 * x * x * x)))


def _iota(shape, dim):
    return lax.broadcasted_iota(jnp.int32, shape, dim)


def _shift_rows(x, x_prev, k, row):
    return jnp.where(row >= k, pltpu.roll(x, k, 0), pltpu.roll(x_prev, k, 0))


def _causal_conv(x, x_prev, w_ref, row):
    y = w_ref[CONV_WIDTH - 1:CONV_WIDTH, :] * x
    for k in range(1, CONV_WIDTH):
        y = y + w_ref[CONV_WIDTH - 1 - k:CONV_WIDTH - k, :] * _shift_rows(x, x_prev, k, row)
    return y


def _unit_lower_inverse(a):
    inv = jnp.where(_iota(a.shape, 0) == _iota(a.shape, 1), 1.0, 0.0) - a
    p = a
    for _ in range(INV_LEVELS - 1):
        p = _dot_hi(p, p)
        inv = inv + _dot_hi(inv, p)
    return inv


def _linear_scan(a, b, row):
    k = 1
    while k < a.shape[0]:
        a_s = jnp.where(row >= k, pltpu.roll(a, k, 0), 1.0)
        b_s = jnp.where(row >= k, pltpu.roll(b, k, 0), 0.0)
        b = a * b_s + b
        a = a * a_s
        k *= 2
    return a, b


def _ffn_kernel(x_ref, nw_ref, wg_ref, wu_ref, wd_ref, fw_ref, o_ref, *, f_chunk, final_norm):
    x = x_ref[...]
    h = _rmsnorm(x, nw_ref[...]).astype(BF16)
    acc = jnp.zeros_like(x)
    for j in range(D_FF // f_chunk):
        lo = j * f_chunk
        g = _dot(h, wg_ref[:, lo:lo + f_chunk])
        u = _dot(h, wu_ref[:, lo:lo + f_chunk])
        acc = acc + _dot((_silu(g) * u).astype(BF16), wd_ref[lo:lo + f_chunk, :])
    y = x + 0.5 * acc
    if final_norm:
        y = _rmsnorm(y, fw_ref[...])
    o_ref[...] = y


def _ffn(x, norm_w, w_gate, w_up, w_down, final_w, *, final_norm, tm=512, f_chunk=256):
    t, d = x.shape
    const = lambda i: (0, 0)
    return pl.pallas_call(
        functools.partial(_ffn_kernel, f_chunk=f_chunk, final_norm=final_norm),
        grid=(t // tm,),
        in_specs=[
            pl.BlockSpec((tm, d), lambda i: (i, 0)),
            pl.BlockSpec((1, d), const),
            pl.BlockSpec((d, D_FF), const),
            pl.BlockSpec((d, D_FF), const),
            pl.BlockSpec((D_FF, d), const),
            pl.BlockSpec((1, d), const),
        ],
        out_specs=pl.BlockSpec((tm, d), lambda i: (i, 0)),
        out_shape=jax.ShapeDtypeStruct((t, d), F32),
        compiler_params=pltpu.CompilerParams(
            dimension_semantics=("arbitrary",), vmem_limit_bytes=VMEM_LIMIT),
        name="ffn",
    )(x, norm_w.reshape(1, d), w_gate.astype(BF16), w_up.astype(BF16), w_down.astype(BF16),
      final_w.reshape(1, d))


def _norm_proj_kernel(x_ref, nw_ref, w_ref, o_ref):
    h = _rmsnorm(x_ref[...], nw_ref[...]).astype(BF16)
    o_ref[...] = _dot(h, w_ref[...])


def _norm_proj(x, norm_w, w, *, tm=512):
    t, d = x.shape
    n = w.shape[1]
    return pl.pallas_call(
        _norm_proj_kernel,
        grid=(t // tm,),
        in_specs=[
            pl.BlockSpec((tm, d), lambda i: (i, 0)),
            pl.BlockSpec((1, d), lambda i: (0, 0)),
            pl.BlockSpec((d, n), lambda i: (0, 0)),
        ],
        out_specs=pl.BlockSpec((tm, n), lambda i: (i, 0)),
        out_shape=jax.ShapeDtypeStruct((t, n), F32),
        compiler_params=pltpu.CompilerParams(
            dimension_semantics=("arbitrary",), vmem_limit_bytes=VMEM_LIMIT),
        name="norm_proj",
    )(x, norm_w.reshape(1, d), w.astype(BF16))


def _out_proj_kernel(x_ref, y_ref, w_ref, o_ref):
    o_ref[...] = x_ref[...] + _dot(y_ref[...].astype(BF16), w_ref[...])


def _out_proj(x, y, w, *, tm=512):
    t, d = x.shape
    k = y.shape[1]
    return pl.pallas_call(
        _out_proj_kernel,
        grid=(t // tm,),
        in_specs=[
            pl.BlockSpec((tm, d), lambda i: (i, 0)),
            pl.BlockSpec((tm, k), lambda i: (i, 0)),
            pl.BlockSpec((k, d), lambda i: (0, 0)),
        ],
        out_specs=pl.BlockSpec((tm, d), lambda i: (i, 0)),
        out_shape=jax.ShapeDtypeStruct((t, d), F32),
        compiler_params=pltpu.CompilerParams(
            dimension_semantics=("arbitrary",), vmem_limit_bytes=VMEM_LIMIT),
        name="out_proj",
    )(x, y, w.astype(BF16))


def _stack(x, m0, m1):
    return jnp.concatenate([x * m0, x * m1], axis=0)


def _unstack(y):
    return y[:CHUNK] + y[CHUNK:]


def _mixer_ab_kernel(z_ref, cw_ref, cb_ref, wa_ref, ba_ref, wx_ref, bx_ref, ap_ref, mu_ref,
                     w0_ref, w2_ref, a0_ref, a2_ref, g2_ref, kk_ref, ka_ref, rk_ref,
                     lnw_ref, lnb_ref, o_ref, zprev_ref, hlru_ref, state_ref):
    n_batch = z_ref.shape[0]
    c = CHUNK
    L = LRU_WIDTH

    @pl.when(pl.program_id(0) == 0)
    def _():
        zprev_ref[...] = jnp.zeros_like(zprev_ref)
        hlru_ref[...] = jnp.zeros_like(hlru_ref)
        state_ref[...] = jnp.zeros_like(state_ref)

    row = _iota((c, 1), 0)
    lane = _iota((1, LANES), 1)
    m0 = jnp.where(lane < RWKV_HEAD_DIM, 1.0, 0.0)
    m1 = 1.0 - m0
    ri = _iota((c, c), 0)
    ci = _iota((c, c), 1)
    l_incl = jnp.where(ri >= ci, 1.0, 0.0)
    si = _iota((LANES, LANES), 0)
    sj = _iota((LANES, LANES), 1)
    same_head = (si // RWKV_HEAD_DIM) == (sj // RWKV_HEAD_DIM)
    ones_bd = jnp.where(same_head, 1.0, 0.0)
    strict_bd = same_head & (si > sj)
    incl_bd = same_head & (si >= sj)
    inv_n = 1.0 / RWKV_HEAD_DIM

    for b in range(n_batch):
        z = z_ref[b]
        zp = zprev_ref[b]

        lx = z[:, :L]
        xc = _causal_conv(lx, zp[:, :L], cw_ref, row) + cb_ref[...]
        xcb = xc.astype(BF16)
        sp_a = _softplus(-ap_ref[...])
        for p in range(N_PAIRS):
            sl = slice(p * LANES, (p + 1) * LANES)
            xcp = xc[:, sl]
            r_gate = _sigmoid(_dot(xcb[:, sl], wa_ref[p]) + ba_ref[:, sl])
            i_gate = _sigmoid(_dot(xcb[:, sl], wx_ref[p]) + bx_ref[:, sl])
            log_a = -LRU_C * r_gate * sp_a[:, sl]
            a = jnp.exp(log_a)
            u = jnp.sqrt(jnp.tanh(-log_a) * (a * a + 1.0)) * (i_gate * xcp)
            a_cum, h = _linear_scan(a, u, row)
            h = h + a_cum * hlru_ref[b, 0:1, sl]
            hlru_ref[b, 0:1, sl] = h[c - 1:c, :]
            o_ref[b, :, sl] = h * _gelu_tanh(z[:, L + p * LANES:L + (p + 1) * LANES])

        s = z[:, 2 * L:]
        pmix = s + mu_ref[...] * (_shift_rows(s, zp[:, 2 * L:], 1, row) - s)
        R = RWKV_WIDTH
        r = pmix[:, :R]
        k = pmix[:, R:2 * R]
        v = pmix[:, 2 * R:3 * R]
        wa_in = pmix[:, 3 * R:3 * R + LANES]
        gl_in = pmix[:, 3 * R + LANES:]
        w_log = -_softplus(-(w0_ref[...] + _dot(jnp.tanh(wa_in).astype(BF16), w2_ref[...]))) - 0.5
        lw = -jnp.exp(w_log)
        a_lr = _sigmoid(a0_ref[...] + _dot(wa_in.astype(BF16), a2_ref[...]))
        g_out = _dot(_sigmoid(gl_in).astype(BF16), g2_ref[...])
        kkr = k * kk_ref[...]
        k_mod = k * (1.0 + (a_lr - 1.0) * ka_ref[...])
        cs = _dot_01_lhs(l_incl, lw)
        cl = cs[c - 1:c, :]
        g_in = jnp.exp(cs)
        g_ex = jnp.exp(cs - lw)
        g_inv = jnp.exp(-cs)
        g_end = jnp.exp(cl - cs)
        g_tot = jnp.exp(cl)

        for p in range(N_PAIRS):
            sl = slice(p * LANES, (p + 1) * LANES)
            rp, vp, ktp = r[:, sl], v[:, sl], k_mod[:, sl]
            kn = kkr[:, sl]
            n2 = _dot_01_rhs(kn * kn, ones_bd)
            kkp = kn / jnp.maximum(jnp.sqrt(n2), 1e-12)
            bp = kkp * a_lr[:, sl]
            kg_st = _stack(kkp * g_ex[:, sl], m0, m1)
            rg = rp * g_in[:, sl]
            bi_st = _stack(bp * g_inv[:, sl], m0, m1)
            ki_st = _stack(ktp * g_inv[:, sl], m0, m1)
            kd_st = _stack(ktp * g_end[:, sl], m0, m1)
            bd_st = _stack(bp * g_end[:, sl], m0, m1)
            v_st = _stack(vp, m0, m1)
            lhs = jnp.concatenate([kg_st, _stack(rg, m0, m1)], axis=0)
            rhs = jnp.concatenate([bi_st, ki_st], axis=0)
            m = _dot_nt(lhs, rhs)
            a_m = jnp.where(strict_bd, m[:2 * c, :2 * c], 0.0)
            b_m = jnp.where(strict_bd, m[:2 * c, 2 * c:], 0.0)
            q_m = jnp.where(incl_bd, m[2 * c:, :2 * c], 0.0)
            p_m = jnp.where(incl_bd, m[2 * c:, 2 * c:], 0.0)
            t_inv = _unit_lower_inverse(a_m)
            bv = _dot_hi(b_m, v_st)
            wu = _dot_hi(t_inv, jnp.concatenate([kg_st, bv], axis=1))
            st = state_ref[b, p]
            u_st = _dot_nt(wu[:, :LANES], st) + wu[:, LANES:]
            y = _dot_nt(rg, st) + _unstack(_dot_hi(p_m, v_st) - _dot_hi(q_m, u_st))
            state_ref[b, p] = st * g_tot[:, sl] + _dot_tn(v_st, kd_st) - _dot_tn(u_st, bd_st)

            mean = _dot_01_rhs(y, ones_bd) * inv_n
            d = y - mean
            var = _dot_01_rhs(d * d, ones_bd) * inv_n
            yn = d * lax.rsqrt(var + RWKV_GN_EPS) * lnw_ref[:, sl] + lnb_ref[:, sl]
            bonus = _dot_01_rhs(rp * ktp * rk_ref[:, sl], ones_bd) * vp
            o_ref[b, :, L + p * LANES:L + (p + 1) * LANES] = (yn + bonus) * g_out[:, sl]

        zprev_ref[b] = z


def _pair_blockdiag(w):
    g = w.reshape(N_PAIRS, PAIR, LRU_BLOCK, LRU_BLOCK)
    z = jnp.zeros((N_PAIRS, LRU_BLOCK, LRU_BLOCK), w.dtype)
    top = jnp.concatenate([g[:, 0], z], axis=2)
    bot = jnp.concatenate([z, g[:, 1]], axis=2)
    return jnp.concatenate([top, bot], axis=1)


def _mixer_ab(z, conv_w, conv_b, ga_w, ga_b, gx_w, gx_b, a_param, mu, w0, w2, a0, a2, g2,
              k_k, k_a, r_k, ln_w, ln_b):
    b, s, n = z.shape
    row = lambda t: t.reshape(1, -1)
    w2p = jnp.concatenate([w2, jnp.zeros((ICLR_RANK, RWKV_WIDTH), F32)], axis=0).astype(BF16)
    a2p = jnp.concatenate([jnp.zeros((DECAY_RANK, RWKV_WIDTH), F32), a2], axis=0).astype(BF16)
    args = (z, conv_w, row(conv_b), _pair_blockdiag(ga_w).astype(BF16), row(ga_b),
            _pair_blockdiag(gx_w).astype(BF16), row(gx_b), row(a_param), row(mu), row(w0), w2p,
            row(a0), a2p, g2.astype(BF16), row(k_k), row(k_a), row(r_k), row(ln_w), row(ln_b))

    def full(a):
        nd = a.ndim
        return pl.BlockSpec(a.shape, lambda i: (0,) * nd)

    return pl.pallas_call(
        _mixer_ab_kernel,
        grid=(s // CHUNK,),
        in_specs=[pl.BlockSpec((b, CHUNK, n), lambda i: (0, i, 0))] + [full(a) for a in args[1:]],
        out_specs=pl.BlockSpec((b, CHUNK, D_MODEL), lambda i: (0, i, 0)),
        out_shape=jax.ShapeDtypeStruct((b, s, D_MODEL), F32),
        scratch_shapes=[
            pltpu.VMEM((b, CHUNK, n), F32),
            pltpu.VMEM((b, 8, LRU_WIDTH), F32),
            pltpu.VMEM((b, N_PAIRS, LANES, LANES), F32),
        ],
        compiler_params=pltpu.CompilerParams(
            dimension_semantics=("arbitrary",), vmem_limit_bytes=VMEM_LIMIT),
        name="mixer_ab",
    )(*args)


def _mixer_c_kernel(z_ref, cw_ref, alog_ref, dt_ref, nw_ref, o_ref, zprev_ref, state_ref):
    n_batch = z_ref.shape[0]
    c = CHUNK
    W = DN_WIDTH
    dh = DN_HEAD_DIM

    @pl.when(pl.program_id(0) == 0)
    def _():
        zprev_ref[...] = jnp.zeros_like(zprev_ref)
        state_ref[...] = jnp.zeros_like(state_ref)

    row = _iota((c, 1), 0)
    ri = _iota((c, c), 0)
    ci = _iota((c, c), 1)
    causal = ri >= ci
    strict = ri > ci
    l_incl = jnp.where(causal, 1.0, 0.0)

    for b in range(n_batch):
        zq = z_ref[b, :, :3 * W]
        qkv = _silu(_causal_conv(zq, zprev_ref[b], cw_ref, row))
        zs = z_ref[b, :, 4 * W:]
        beta_all = _sigmoid(zs)
        g_all = -jnp.exp(alog_ref[...]) * _softplus(zs + dt_ref[...])
        gc_all = _dot_01_lhs(l_incl, g_all)
        gc_rows = gc_all.T
        for h in range(DN_HEADS):
            sl = slice(h * dh, (h + 1) * dh)
            beta = beta_all[:, h:h + 1]
            gcol = gc_all[:, DN_HEADS + h:DN_HEADS + h + 1]
            grow = gc_rows[DN_HEADS + h:DN_HEADS + h + 1, :]
            glast = gcol[c - 1:c, :]
            qh = qkv[:, sl]
            kh = qkv[:, W + h * dh:W + (h + 1) * dh]
            vh = qkv[:, 2 * W + h * dh:2 * W + (h + 1) * dh]
            qh = qh * lax.rsqrt(jnp.sum(qh * qh, axis=-1, keepdims=True) + 1e-6) * (dh ** -0.5)
            kh = kh * lax.rsqrt(jnp.sum(kh * kh, axis=-1, keepdims=True) + 1e-6)
            decay = jnp.exp(jnp.where(causal, gcol - grow, -jnp.inf))
            a_m = jnp.where(strict, beta * _dot_nt(kh, kh) * decay, 0.0)
            t_inv = _unit_lower_inverse(a_m)
            egc = jnp.exp(gcol)
            sol = _dot_hi(t_inv, jnp.concatenate([vh * beta, kh * (beta * egc)], axis=1))
            u = sol[:, :dh]
            w = sol[:, dh:]
            attn = _dot_nt(qh, kh) * decay
            st = state_ref[b, h]
            v_new = u - _dot_hi(w, st)
            o = _dot_hi(qh * egc, st) + _dot_hi(attn, v_new)
            state_ref[b, h] = st * jnp.exp(glast) + _dot_tn(kh * jnp.exp(glast - gcol), v_new)
            gate = z_ref[b, :, 3 * W + h * dh:3 * W + (h + 1) * dh]
            o_ref[b, :, sl] = _rmsnorm(o, nw_ref[...]) * _silu(gate)
        zprev_ref[b] = zq


def _mixer_c(z, conv_w, a_log, dt_bias, norm_w):
    b, s, n = z.shape
    pad = jnp.zeros((C_SMALL - 2 * DN_HEADS,), F32)
    lead = jnp.zeros((DN_HEADS,), F32)
    alog_row = jnp.concatenate([lead, a_log, pad]).reshape(1, C_SMALL)
    dt_row = jnp.concatenate([lead, dt_bias, pad]).reshape(1, C_SMALL)
    args = (z, conv_w, alog_row, dt_row, norm_w.reshape(1, DN_HEAD_DIM))

    def full(a):
        nd = a.ndim
        return pl.BlockSpec(a.shape, lambda i: (0,) * nd)

    return pl.pallas_call(
        _mixer_c_kernel,
        grid=(s // CHUNK,),
        in_specs=[pl.BlockSpec((b, CHUNK, n), lambda i: (0, i, 0))] + [full(a) for a in args[1:]],
        out_specs=pl.BlockSpec((b, CHUNK, DN_WIDTH), lambda i: (0, i, 0)),
        out_shape=jax.ShapeDtypeStruct((b, s, DN_WIDTH), F32),
        scratch_shapes=[
            pltpu.VMEM((b, CHUNK, 3 * DN_WIDTH), F32),
            pltpu.VMEM((b, DN_HEADS, DN_HEAD_DIM, DN_HEAD_DIM), F32),
        ],
        compiler_params=pltpu.CompilerParams(
            dimension_semantics=("arbitrary",), vmem_limit_bytes=VMEM_LIMIT),
        name="mixer_c",
    )(*args)


def kernel(x, ffn_norm, ffn_w_gate, ffn_w_up, ffn_w_down, mix_norm, final_norm, ab_w_in, lru_conv_w, lru_conv_b, lru_gate_a_w, lru_gate_a_b, lru_gate_x_w, lru_gate_x_b, lru_a_param, rwkv_mu, rwkv_w0, rwkv_w2, rwkv_a0, rwkv_a2, rwkv_g2, rwkv_k_k, rwkv_k_a, rwkv_r_k, rwkv_ln_w, rwkv_ln_b, ab_w_out, c_w_in, dn_conv_w, dn_A_log, dn_dt_bias, dn_norm, c_w_out):
    b, s, d = x.shape
    depth = ffn_norm.shape[0]
    xf = x.reshape(b * s, d)
    for l in range(depth):
        i = l // 2
        xf = _ffn(xf, ffn_norm[l, 0], ffn_w_gate[l, 0], ffn_w_up[l, 0], ffn_w_down[l, 0],
                  final_norm, final_norm=False)
        if l % 2 == 0:
            z = _norm_proj(xf, mix_norm[l], ab_w_in[i]).reshape(b, s, AB_IN)
            y = _mixer_ab(z, lru_conv_w[i], lru_conv_b[i], lru_gate_a_w[i], lru_gate_a_b[i],
                          lru_gate_x_w[i], lru_gate_x_b[i], lru_a_param[i], rwkv_mu[i],
                          rwkv_w0[i], rwkv_w2[i], rwkv_a0[i], rwkv_a2[i], rwkv_g2[i],
                          rwkv_k_k[i], rwkv_k_a[i], rwkv_r_k[i], rwkv_ln_w[i], rwkv_ln_b[i])
            xf = _out_proj(xf, y.reshape(b * s, D_MODEL), ab_w_out[i])
        else:
            w_in = jnp.concatenate(
                [c_w_in[i], jnp.zeros((d, C_IN_PAD - c_w_in.shape[2]), F32)], axis=1)
            z = _norm_proj(xf, mix_norm[l], w_in).reshape(b, s, C_IN_PAD)
            y = _mixer_c(z, dn_conv_w[i], dn_A_log[i], dn_dt_bias[i], dn_norm[i])
            xf = _out_proj(xf, y.reshape(b * s, DN_WIDTH), c_w_out[i])
        xf = _ffn(xf, ffn_norm[l, 1], ffn_w_gate[l, 1], ffn_w_up[l, 1], ffn_w_down[l, 1],
                  final_norm, final_norm=(l == depth - 1))
    return xf.reshape(b, s, d)
```

```python
import functools

import jax
import jax.numpy as jnp
from jax import lax
from jax.experimental import pallas as pl
from jax.experimental.pallas import tpu as pltpu

F32 = jnp.float32
BF16 = jnp.bfloat16
HIGHEST = lax.Precision.HIGHEST

D_MODEL = 1024
D_FF = 2816
NORM_EPS = 1e-6
CONV_WIDTH = 4

LRU_WIDTH = 512
LRU_BLOCK = 64
LRU_C = 8.0

RWKV_WIDTH = 512
RWKV_HEAD_DIM = 64
DECAY_RANK = 64
ICLR_RANK = 64
GATE_RANK = 128
RWKV_GN_EPS = 64e-5
SHIFT_WIDTH = 3 * RWKV_WIDTH + DECAY_RANK + ICLR_RANK + GATE_RANK
AB_IN = 2 * LRU_WIDTH + SHIFT_WIDTH

DN_HEADS = 8
DN_HEAD_DIM = 128
DN_WIDTH = DN_HEADS * DN_HEAD_DIM
C_SMALL = 128
C_IN_PAD = 4 * DN_WIDTH + C_SMALL

LANES = 128
CHUNK = 64
PAIR = LANES // RWKV_HEAD_DIM
N_PAIRS = RWKV_WIDTH // LANES
INV_LEVELS = 6
VMEM_LIMIT = 56 * 1024 * 1024


def _dot(a, b):
    return jnp.dot(a, b, preferred_element_type=F32)


REC_PASSES = 1

_NN = (((1,), (0,)), ((), ()))
_NT = (((1,), (1,)), ((), ()))
_TN = (((0,), (0,)), ((), ()))


def _split(x, parts):
    out = []
    for _ in range(parts - 1):
        hi = x.astype(BF16)
        out.append(hi)
        x = x - hi.astype(F32)
    out.append(x.astype(BF16))
    return out


def _rec_dot(a, b, dims):
    if REC_PASSES == 6:
        return lax.dot_general(a, b, dims, precision=HIGHEST, preferred_element_type=F32)
    dg = lambda x, y: lax.dot_general(x, y, dims, preferred_element_type=F32)
    if REC_PASSES == 1:
        return dg(a.astype(BF16), b.astype(BF16))
    a_hi, a_lo = _split(a, 2)
    b_hi, b_lo = _split(b, 2)
    return dg(a_hi, b_hi) + dg(a_hi, b_lo) + dg(a_lo, b_hi)


def _dot_hi(a, b):
    return _rec_dot(a, b, _NN)


def _dot_nt(a, b):
    return _rec_dot(a, b, _NT)


def _dot_tn(a, b):
    return _rec_dot(a, b, _TN)


def _dot_01_lhs(l, x):
    lb = l.astype(BF16)
    return sum(jnp.dot(lb, xp, preferred_element_type=F32) for xp in _split(x, 3))


def _dot_01_rhs(x, r):
    rb = r.astype(BF16)
    return sum(jnp.dot(xp, rb, preferred_element_type=F32) for xp in _split(x, 3))


def _rmsnorm(x, w, eps=NORM_EPS):
    return x * lax.rsqrt(jnp.mean(x * x, axis=-1, keepdims=True) + eps) * w


def _sigmoid(x):
    return 1.0 / (1.0 + jnp.exp(-x))


def _silu(x):
    return x * _sigmoid(x)


def _softplus(x):
    return jnp.maximum(x, 0.0) + jnp.log(1.0 + jnp.exp(-jnp.abs(x)))


def _gelu_tanh(x):
    return 0.5 * x * (1.0 + jnp.tanh(0.7978845608028654 * (x + 0.044715 * x * x * x)))


def _iota(shape, dim):
    return lax.broadcasted_iota(jnp.int32, shape, dim)


def _shift_rows(x, x_prev, k, row):
    return jnp.where(row >= k, pltpu.roll(x, k, 0), pltpu.roll(x_prev, k, 0))


def _causal_conv(x, x_prev, w_ref, row):
    y = w_ref[CONV_WIDTH - 1:CONV_WIDTH, :] * x
    for k in range(1, CONV_WIDTH):
        y = y + w_ref[CONV_WIDTH - 1 - k:CONV_WIDTH - k, :] * _shift_rows(x, x_prev, k, row)
    return y


def _unit_lower_inverse(a):
    inv = jnp.where(_iota(a.shape, 0) == _iota(a.shape, 1), 1.0, 0.0) - a
    p = a
    for _ in range(INV_LEVELS - 1):
        p = _dot_hi(p, p)
        inv = inv + _dot_hi(inv, p)
    return inv


def _unit_lower_inverse_many(mats):
    eye = jnp.where(_iota(mats[0].shape, 0) == _iota(mats[0].shape, 1), 1.0, 0.0)
    inv = [eye - a for a in mats]
    p = mats
    for _ in range(INV_LEVELS - 1):
        p = [_dot_hi(x, x) for x in p]
        inv = [i + _dot_hi(i, x) for i, x in zip(inv, p)]
    return inv


def _linear_scan(a, b, row):
    k = 1
    while k < a.shape[0]:
        a_s = jnp.where(row >= k, pltpu.roll(a, k, 0), 1.0)
        b_s = jnp.where(row >= k, pltpu.roll(b, k, 0), 0.0)
        b = a * b_s + b
        a = a * a_s
        k *= 2
    return a, b


def _ffn_kernel(x_ref, nw_ref, wg_ref, wu_ref, wd_ref, fw_ref, o_ref, *, f_chunk, final_norm):
    x = x_ref[...]
    h = _rmsnorm(x, nw_ref[...]).astype(BF16)
    acc = jnp.zeros_like(x)
    for j in range(D_FF // f_chunk):
        lo = j * f_chunk
        g = _dot(h, wg_ref[:, lo:lo + f_chunk])
        u = _dot(h, wu_ref[:, lo:lo + f_chunk])
        acc = acc + _dot((_silu(g) * u).astype(BF16), wd_ref[lo:lo + f_chunk, :])
    y = x + 0.5 * acc
    if final_norm:
        y = _rmsnorm(y, fw_ref[...])
    o_ref[...] = y


def _ffn(x, norm_w, w_gate, w_up, w_down, final_w, *, final_norm, tm=512, f_chunk=256):
    t, d = x.shape
    const = lambda i: (0, 0)
    return pl.pallas_call(
        functools.partial(_ffn_kernel, f_chunk=f_chunk, final_norm=final_norm),
        grid=(t // tm,),
        in_specs=[
            pl.BlockSpec((tm, d), lambda i: (i, 0)),
            pl.BlockSpec((1, d), const),
            pl.BlockSpec((d, D_FF), const),
            pl.BlockSpec((d, D_FF), const),
            pl.BlockSpec((D_FF, d), const),
            pl.BlockSpec((1, d), const),
        ],
        out_specs=pl.BlockSpec((tm, d), lambda i: (i, 0)),
        out_shape=jax.ShapeDtypeStruct((t, d), F32),
        compiler_params=pltpu.CompilerParams(
            dimension_semantics=("arbitrary",), vmem_limit_bytes=VMEM_LIMIT),
        name="ffn",
    )(x, norm_w.reshape(1, d), w_gate.astype(BF16), w_up.astype(BF16), w_down.astype(BF16),
      final_w.reshape(1, d))


def _norm_proj_kernel(x_ref, nw_ref, w_ref, o_ref):
    h = _rmsnorm(x_ref[...], nw_ref[...]).astype(BF16)
    o_ref[...] = _dot(h, w_ref[...])


def _norm_proj(x, norm_w, w, *, tm=512):
    t, d = x.shape
    n = w.shape[1]
    return pl.pallas_call(
        _norm_proj_kernel,
        grid=(t // tm,),
        in_specs=[
            pl.BlockSpec((tm, d), lambda i: (i, 0)),
            pl.BlockSpec((1, d), lambda i: (0, 0)),
            pl.BlockSpec((d, n), lambda i: (0, 0)),
        ],
        out_specs=pl.BlockSpec((tm, n), lambda i: (i, 0)),
        out_shape=jax.ShapeDtypeStruct((t, n), F32),
        compiler_params=pltpu.CompilerParams(
            dimension_semantics=("arbitrary",), vmem_limit_bytes=VMEM_LIMIT),
        name="norm_proj",
    )(x, norm_w.reshape(1, d), w.astype(BF16))


def _out_proj_kernel(x_ref, y_ref, w_ref, o_ref):
    o_ref[...] = x_ref[...] + _dot(y_ref[...].astype(BF16), w_ref[...])


def _out_proj(x, y, w, *, tm=512):
    t, d = x.shape
    k = y.shape[1]
    return pl.pallas_call(
        _out_proj_kernel,
        grid=(t // tm,),
        in_specs=[
            pl.BlockSpec((tm, d), lambda i: (i, 0)),
            pl.BlockSpec((tm, k), lambda i: (i, 0)),
            pl.BlockSpec((k, d), lambda i: (0, 0)),
        ],
        out_specs=pl.BlockSpec((tm, d), lambda i: (i, 0)),
        out_shape=jax.ShapeDtypeStruct((t, d), F32),
        compiler_params=pltpu.CompilerParams(
            dimension_semantics=("arbitrary",), vmem_limit_bytes=VMEM_LIMIT),
        name="out_proj",
    )(x, y, w.astype(BF16))


def _stack(x, m0, m1):
    return jnp.concatenate([x * m0, x * m1], axis=0)


def _unstack(y):
    return y[:CHUNK] + y[CHUNK:]


def _mixer_ab_kernel(z_ref, cw_ref, cb_ref, wa_ref, ba_ref, wx_ref, bx_ref, ap_ref, mu_ref,
                     w0_ref, w2_ref, a0_ref, a2_ref, g2_ref, kk_ref, ka_ref, rk_ref,
                     lnw_ref, lnb_ref, o_ref, zprev_ref, hlru_ref, state_ref):
    n_batch = z_ref.shape[0]
    c = CHUNK
    L = LRU_WIDTH

    @pl.when(pl.program_id(0) == 0)
    def _():
        zprev_ref[...] = jnp.zeros_like(zprev_ref)
        hlru_ref[...] = jnp.zeros_like(hlru_ref)
        state_ref[...] = jnp.zeros_like(state_ref)

    row = _iota((c, 1), 0)
    lane = _iota((1, LANES), 1)
    m0 = jnp.where(lane < RWKV_HEAD_DIM, 1.0, 0.0)
    m1 = 1.0 - m0
    ri = _iota((c, c), 0)
    ci = _iota((c, c), 1)
    l_incl = jnp.where(ri >= ci, 1.0, 0.0)
    si = _iota((LANES, LANES), 0)
    sj = _iota((LANES, LANES), 1)
    same_head = (si // RWKV_HEAD_DIM) == (sj // RWKV_HEAD_DIM)
    ones_bd = jnp.where(same_head, 1.0, 0.0)
    strict_bd = same_head & (si > sj)
    incl_bd = same_head & (si >= sj)
    inv_n = 1.0 / RWKV_HEAD_DIM

    nb = range(n_batch)
    groups = [(b, p) for b in nb for p in range(N_PAIRS)]
    tile = lambda x, p: x[:, p * LANES:(p + 1) * LANES]
    z = [z_ref[b] for b in nb]
    zp = [zprev_ref[b] for b in nb]

    xc = [_causal_conv(z[b][:, :L], zp[b][:, :L], cw_ref, row) + cb_ref[...] for b in nb]
    xcb = [x.astype(BF16) for x in xc]
    sp_a = _softplus(-ap_ref[...])
    r_gate = [_sigmoid(_dot(tile(xcb[b], p), wa_ref[p]) + tile(ba_ref, p)) for b, p in groups]
    i_gate = [_sigmoid(_dot(tile(xcb[b], p), wx_ref[p]) + tile(bx_ref, p)) for b, p in groups]
    log_a = [-LRU_C * rg_ * tile(sp_a, p) for rg_, (b, p) in zip(r_gate, groups)]
    a = [jnp.exp(x) for x in log_a]
    u = [jnp.sqrt(jnp.tanh(-la) * (a_ * a_ + 1.0)) * (ig * tile(xc[b], p))
         for la, a_, ig, (b, p) in zip(log_a, a, i_gate, groups)]
    scans = [_linear_scan(a_, u_, row) for a_, u_ in zip(a, u)]
    for (a_cum, h), (b, p) in zip(scans, groups):
        sl = slice(p * LANES, (p + 1) * LANES)
        h = h + a_cum * hlru_ref[b, 0:1, sl]
        hlru_ref[b, 0:1, sl] = h[c - 1:c, :]
        o_ref[b, :, sl] = h * _gelu_tanh(z[b][:, L + p * LANES:L + (p + 1) * LANES])

    R = RWKV_WIDTH
    s = [x[:, 2 * L:] for x in z]
    pmix = [s[b] + mu_ref[...] * (_shift_rows(s[b], zp[b][:, 2 * L:], 1, row) - s[b]) for b in nb]
    r = [x[:, :R] for x in pmix]
    k = [x[:, R:2 * R] for x in pmix]
    v = [x[:, 2 * R:3 * R] for x in pmix]
    wa_in = [x[:, 3 * R:3 * R + LANES] for x in pmix]
    gl_in = [x[:, 3 * R + LANES:] for x in pmix]
    w_log = [-_softplus(-(w0_ref[...] + _dot(jnp.tanh(x).astype(BF16), w2_ref[...]))) - 0.5
             for x in wa_in]
    lw = [-jnp.exp(x) for x in w_log]
    a_lr = [_sigmoid(a0_ref[...] + _dot(x.astype(BF16), a2_ref[...])) for x in wa_in]
    g_out = [_dot(_sigmoid(x).astype(BF16), g2_ref[...]) for x in gl_in]
    kkr = [x * kk_ref[...] for x in k]
    k_mod = [kx * (1.0 + (al - 1.0) * ka_ref[...]) for kx, al in zip(k, a_lr)]
    cs = [_dot_01_lhs(l_incl, x) for x in lw]
    cl = [x[c - 1:c, :] for x in cs]
    g_in = [jnp.exp(x) for x in cs]
    g_ex = [jnp.exp(x - y) for x, y in zip(cs, lw)]
    g_inv = [jnp.exp(-x) for x in cs]
    g_end = [jnp.exp(x - y) for x, y in zip(cl, cs)]
    g_tot = [jnp.exp(x) for x in cl]

    G = lambda xs: [tile(xs[b], p) for b, p in groups]
    rp, vp, ktp, kn, alp = G(r), G(v), G(k_mod), G(kkr), G(a_lr)
    gin, gex, ginv, gend, gtot = G(g_in), G(g_ex), G(g_inv), G(g_end), G(g_tot)
    n2 = [_dot_01_rhs(x * x, ones_bd) for x in kn]
    kkp = [x / jnp.maximum(jnp.sqrt(n), 1e-12) for x, n in zip(kn, n2)]
    bp = [x * y for x, y in zip(kkp, alp)]
    stk = lambda x: _stack(x, m0, m1)
    kg_st = [stk(x * y) for x, y in zip(kkp, gex)]
    rg = [x * y for x, y in zip(rp, gin)]
    bi_st = [stk(x * y) for x, y in zip(bp, ginv)]
    ki_st = [stk(x * y) for x, y in zip(ktp, ginv)]
    kd_st = [stk(x * y) for x, y in zip(ktp, gend)]
    bd_st = [stk(x * y) for x, y in zip(bp, gend)]
    v_st = [stk(x) for x in vp]
    m = [_dot_nt(jnp.concatenate([kg, stk(rg_)], axis=0), jnp.concatenate([bi, ki], axis=0))
         for kg, rg_, bi, ki in zip(kg_st, rg, bi_st, ki_st)]
    a_m = [jnp.where(strict_bd, x[:2 * c, :2 * c], 0.0) for x in m]
    b_m = [jnp.where(strict_bd, x[:2 * c, 2 * c:], 0.0) for x in m]
    q_m = [jnp.where(incl_bd, x[2 * c:, :2 * c], 0.0) for x in m]
    p_m = [jnp.where(incl_bd, x[2 * c:, 2 * c:], 0.0) for x in m]
    t_inv = _unit_lower_inverse_many(a_m)
    bv = [_dot_hi(x, y) for x, y in zip(b_m, v_st)]
    wu = [_dot_hi(t, jnp.concatenate([kg, x], axis=1)) for t, kg, x in zip(t_inv, kg_st, bv)]
    pv = [_dot_hi(x, y) for x, y in zip(p_m, v_st)]
    vkd = [_dot_tn(x, y) for x, y in zip(v_st, kd_st)]
    st = [state_ref[b, p] for b, p in groups]
    u_st = [_dot_nt(x[:, :LANES], s_) + x[:, LANES:] for x, s_ in zip(wu, st)]
    rs = [_dot_nt(x, s_) for x, s_ in zip(rg, st)]
    qu = [_dot_hi(x, y) for x, y in zip(q_m, u_st)]
    ubd = [_dot_tn(x, y) for x, y in zip(u_st, bd_st)]
    y = [x + _unstack(p_ - q_) for x, p_, q_ in zip(rs, pv, qu)]
    for i, (b, p) in enumerate(groups):
        state_ref[b, p] = st[i] * gtot[i] + vkd[i] - ubd[i]

    mean = [_dot_01_rhs(x, ones_bd) * inv_n for x in y]
    d = [x - mu_ for x, mu_ in zip(y, mean)]
    var = [_dot_01_rhs(x * x, ones_bd) * inv_n for x in d]
    bonus = [_dot_01_rhs(r_ * kt * tile(rk_ref, p), ones_bd) * v_
             for r_, kt, v_, (b, p) in zip(rp, ktp, vp, groups)]
    for i, (b, p) in enumerate(groups):
        yn = d[i] * lax.rsqrt(var[i] + RWKV_GN_EPS) * tile(lnw_ref, p) + tile(lnb_ref, p)
        o_ref[b, :, L + p * LANES:L + (p + 1) * LANES] = (yn + bonus[i]) * tile(g_out[b], p)
    for b in nb:
        zprev_ref[b] = z[b]


def _pair_blockdiag(w):
    g = w.reshape(N_PAIRS, PAIR, LRU_BLOCK, LRU_BLOCK)
    z = jnp.zeros((N_PAIRS, LRU_BLOCK, LRU_BLOCK), w.dtype)
    top = jnp.concatenate([g[:, 0], z], axis=2)
    bot = jnp.concatenate([z, g[:, 1]], axis=2)
    return jnp.concatenate([top, bot], axis=1)


def _mixer_ab(z, conv_w, conv_b, ga_w, ga_b, gx_w, gx_b, a_param, mu, w0, w2, a0, a2, g2,
              k_k, k_a, r_k, ln_w, ln_b):
    b, s, n = z.shape
    row = lambda t: t.reshape(1, -1)
    w2p = jnp.concatenate([w2, jnp.zeros((ICLR_RANK, RWKV_WIDTH), F32)], axis=0).astype(BF16)
    a2p = jnp.concatenate([jnp.zeros((DECAY_RANK, RWKV_WIDTH), F32), a2], axis=0).astype(BF16)
    args = (z, conv_w, row(conv_b), _pair_blockdiag(ga_w).astype(BF16), row(ga_b),
            _pair_blockdiag(gx_w).astype(BF16), row(gx_b), row(a_param), row(mu), row(w0), w2p,
            row(a0), a2p, g2.astype(BF16), row(k_k), row(k_a), row(r_k), row(ln_w), row(ln_b))

    def full(a):
        nd = a.ndim
        return pl.BlockSpec(a.shape, lambda i: (0,) * nd)

    return pl.pallas_call(
        _mixer_ab_kernel,
        grid=(s // CHUNK,),
        in_specs=[pl.BlockSpec((b, CHUNK, n), lambda i: (0, i, 0))] + [full(a) for a in args[1:]],
        out_specs=pl.BlockSpec((b, CHUNK, D_MODEL), lambda i: (0, i, 0)),
        out_shape=jax.ShapeDtypeStruct((b, s, D_MODEL), F32),
        scratch_shapes=[
            pltpu.VMEM((b, CHUNK, n), F32),
            pltpu.VMEM((b, 8, LRU_WIDTH), F32),
            pltpu.VMEM((b, N_PAIRS, LANES, LANES), F32),
        ],
        compiler_params=pltpu.CompilerParams(
            dimension_semantics=("arbitrary",), vmem_limit_bytes=VMEM_LIMIT),
        name="mixer_ab",
    )(*args)


def _mixer_c_kernel(z_ref, cw_ref, alog_ref, dt_ref, nw_ref, o_ref, zprev_ref, state_ref):
    n_batch = z_ref.shape[0]
    c = CHUNK
    W = DN_WIDTH
    dh = DN_HEAD_DIM

    @pl.when(pl.program_id(0) == 0)
    def _():
        zprev_ref[...] = jnp.zeros_like(zprev_ref)
        state_ref[...] = jnp.zeros_like(state_ref)

    row = _iota((c, 1), 0)
    ri = _iota((c, c), 0)
    ci = _iota((c, c), 1)
    causal = ri >= ci
    strict = ri > ci
    l_incl = jnp.where(causal, 1.0, 0.0)

    groups = [(b, h) for b in range(n_batch) for h in range(DN_HEADS)]
    zq = [z_ref[b, :, :3 * W] for b in range(n_batch)]
    qkv = [_silu(_causal_conv(zq[b], zprev_ref[b], cw_ref, row)) for b in range(n_batch)]
    zs = [z_ref[b, :, 4 * W:] for b in range(n_batch)]
    beta_all = [_sigmoid(x) for x in zs]
    g_all = [-jnp.exp(alog_ref[...]) * _softplus(x + dt_ref[...]) for x in zs]
    gc_all = [_dot_01_lhs(l_incl, x) for x in g_all]
    gc_rows = [x.T for x in gc_all]

    beta = [beta_all[b][:, h:h + 1] for b, h in groups]
    gcol = [gc_all[b][:, DN_HEADS + h:DN_HEADS + h + 1] for b, h in groups]
    grow = [gc_rows[b][DN_HEADS + h:DN_HEADS + h + 1, :] for b, h in groups]
    glast = [x[c - 1:c, :] for x in gcol]
    q = [qkv[b][:, h * dh:(h + 1) * dh] for b, h in groups]
    k = [qkv[b][:, W + h * dh:W + (h + 1) * dh] for b, h in groups]
    v = [qkv[b][:, 2 * W + h * dh:2 * W + (h + 1) * dh] for b, h in groups]
    q = [x * lax.rsqrt(jnp.sum(x * x, axis=-1, keepdims=True) + 1e-6) * (dh ** -0.5) for x in q]
    k = [x * lax.rsqrt(jnp.sum(x * x, axis=-1, keepdims=True) + 1e-6) for x in k]
    decay = [jnp.exp(jnp.where(causal, gc - gr, -jnp.inf)) for gc, gr in zip(gcol, grow)]
    kk = [_dot_nt(x, x) for x in k]
    a_m = [jnp.where(strict, bt * m * d, 0.0) for bt, m, d in zip(beta, kk, decay)]
    t_inv = _unit_lower_inverse_many(a_m)
    egc = [jnp.exp(x) for x in gcol]
    rhs = [jnp.concatenate([vv * bt, kx * (bt * e)], axis=1)
           for vv, kx, bt, e in zip(v, k, beta, egc)]
    sol = [_dot_hi(t, r) for t, r in zip(t_inv, rhs)]
    attn = [_dot_nt(qx, kx) * d for qx, kx, d in zip(q, k, decay)]
    st = [state_ref[b, h] for b, h in groups]
    ws = [_dot_hi(x[:, dh:], s_) for x, s_ in zip(sol, st)]
    v_new = [x[:, :dh] - y for x, y in zip(sol, ws)]
    qs = [_dot_hi(qx * e, s_) for qx, e, s_ in zip(q, egc, st)]
    av = [_dot_hi(a, vn) for a, vn in zip(attn, v_new)]
    kv = [_dot_tn(kx * jnp.exp(gl - gc), vn) for kx, gl, gc, vn in zip(k, glast, gcol, v_new)]
    for i, (b, h) in enumerate(groups):
        state_ref[b, h] = st[i] * jnp.exp(glast[i]) + kv[i]
        gate = z_ref[b, :, 3 * W + h * dh:3 * W + (h + 1) * dh]
        o_ref[b, :, h * dh:(h + 1) * dh] = _rmsnorm(qs[i] + av[i], nw_ref[...]) * _silu(gate)
    for b in range(n_batch):
        zprev_ref[b] = zq[b]


def _mixer_c(z, conv_w, a_log, dt_bias, norm_w):
    b, s, n = z.shape
    pad = jnp.zeros((C_SMALL - 2 * DN_HEADS,), F32)
    lead = jnp.zeros((DN_HEADS,), F32)
    alog_row = jnp.concatenate([lead, a_log, pad]).reshape(1, C_SMALL)
    dt_row = jnp.concatenate([lead, dt_bias, pad]).reshape(1, C_SMALL)
    args = (z, conv_w, alog_row, dt_row, norm_w.reshape(1, DN_HEAD_DIM))

    def full(a):
        nd = a.ndim
        return pl.BlockSpec(a.shape, lambda i: (0,) * nd)

    return pl.pallas_call(
        _mixer_c_kernel,
        grid=(s // CHUNK,),
        in_specs=[pl.BlockSpec((b, CHUNK, n), lambda i: (0, i, 0))] + [full(a) for a in args[1:]],
        out_specs=pl.BlockSpec((b, CHUNK, DN_WIDTH), lambda i: (0, i, 0)),
        out_shape=jax.ShapeDtypeStruct((b, s, DN_WIDTH), F32),
        scratch_shapes=[
            pltpu.VMEM((b, CHUNK, 3 * DN_WIDTH), F32),
            pltpu.VMEM((b, DN_HEADS, DN_HEAD_DIM, DN_HEAD_DIM), F32),
        ],
        compiler_params=pltpu.CompilerParams(
            dimension_semantics=("arbitrary",), vmem_limit_bytes=VMEM_LIMIT),
        name="mixer_c",
    )(*args)


def kernel(x, ffn_norm, ffn_w_gate, ffn_w_up, ffn_w_down, mix_norm, final_norm, ab_w_in, lru_conv_w, lru_conv_b, lru_gate_a_w, lru_gate_a_b, lru_gate_x_w, lru_gate_x_b, lru_a_param, rwkv_mu, rwkv_w0, rwkv_w2, rwkv_a0, rwkv_a2, rwkv_g2, rwkv_k_k, rwkv_k_a, rwkv_r_k, rwkv_ln_w, rwkv_ln_b, ab_w_out, c_w_in, dn_conv_w, dn_A_log, dn_dt_bias, dn_norm, c_w_out):
    b, s, d = x.shape
    depth = ffn_norm.shape[0]
    xf = x.reshape(b * s, d)
    for l in range(depth):
        i = l // 2
        xf = _ffn(xf, ffn_norm[l, 0], ffn_w_gate[l, 0], ffn_w_up[l, 0], ffn_w_down[l, 0],
                  final_norm, final_norm=False)
        if l % 2 == 0:
            z = _norm_proj(xf, mix_norm[l], ab_w_in[i]).reshape(b, s, AB_IN)
            y = _mixer_ab(z, lru_conv_w[i], lru_conv_b[i], lru_gate_a_w[i], lru_gate_a_b[i],
                          lru_gate_x_w[i], lru_gate_x_b[i], lru_a_param[i], rwkv_mu[i],
                          rwkv_w0[i], rwkv_w2[i], rwkv_a0[i], rwkv_a2[i], rwkv_g2[i],
                          rwkv_k_k[i], rwkv_k_a[i], rwkv_r_k[i], rwkv_ln_w[i], rwkv_ln_b[i])
            xf = _out_proj(xf, y.reshape(b * s, D_MODEL), ab_w_out[i])
        else:
            w_in = jnp.concatenate(
                [c_w_in[i], jnp.zeros((d, C_IN_PAD - c_w_in.shape[2]), F32)], axis=1)
            z = _norm_proj(xf, mix_norm[l], w_in).reshape(b, s, C_IN_PAD)
            y = _mixer_c(z, dn_conv_w[i], dn_A_log[i], dn_dt_bias[i], dn_norm[i])
            xf = _out_proj(xf, y.reshape(b * s, DN_WIDTH), c_w_out[i])
        xf = _ffn(xf, ffn_norm[l, 1], ffn_w_gate[l, 1], ffn_w_up[l, 1], ffn_w_down[l, 1],
                  final_norm, final_norm=(l == depth - 1))
    return xf.reshape(b, s, d)
```

```python
import functools

import jax
import jax.numpy as jnp
from jax import lax
from jax.experimental import pallas as pl
from jax.experimental.pallas import tpu as pltpu

F32 = jnp.float32
BF16 = jnp.bfloat16
HIGHEST = lax.Precision.HIGHEST

D_MODEL = 1024
D_FF = 2816
NORM_EPS = 1e-6
CONV_WIDTH = 4

LRU_WIDTH = 512
LRU_BLOCK = 64
LRU_C = 8.0

RWKV_WIDTH = 512
RWKV_HEAD_DIM = 64
DECAY_RANK = 64
ICLR_RANK = 64
GATE_RANK = 128
RWKV_GN_EPS = 64e-5
SHIFT_WIDTH = 3 * RWKV_WIDTH + DECAY_RANK + ICLR_RANK + GATE_RANK
AB_IN = 2 * LRU_WIDTH + SHIFT_WIDTH

DN_HEADS = 8
DN_HEAD_DIM = 128
DN_WIDTH = DN_HEADS * DN_HEAD_DIM
C_SMALL = 128
C_IN_PAD = 4 * DN_WIDTH + C_SMALL

LANES = 128
CHUNK = 64
PAIR = LANES // RWKV_HEAD_DIM
N_PAIRS = RWKV_WIDTH // LANES
INV_LEVELS = 6
VMEM_LIMIT = 56 * 1024 * 1024


def _dot(a, b):
    return jnp.dot(a, b, preferred_element_type=F32)


REC_PASSES = 1

_NN = (((1,), (0,)), ((), ()))
_NT = (((1,), (1,)), ((), ()))
_TN = (((0,), (0,)), ((), ()))


def _split(x, parts):
    out = []
    for _ in range(parts - 1):
        hi = x.astype(BF16)
        out.append(hi)
        x = x - hi.astype(F32)
    out.append(x.astype(BF16))
    return out


def _rec_dot(a, b, dims):
    if REC_PASSES == 6:
        return lax.dot_general(a, b, dims, precision=HIGHEST, preferred_element_type=F32)
    dg = lambda x, y: lax.dot_general(x, y, dims, preferred_element_type=F32)
    if REC_PASSES == 1:
        return dg(a.astype(BF16), b.astype(BF16))
    a_hi, a_lo = _split(a, 2)
    b_hi, b_lo = _split(b, 2)
    return dg(a_hi, b_hi) + dg(a_hi, b_lo) + dg(a_lo, b_hi)


def _dot_hi(a, b):
    return _rec_dot(a, b, _NN)


def _dot_nt(a, b):
    return _rec_dot(a, b, _NT)


def _dot_tn(a, b):
    return _rec_dot(a, b, _TN)


def _cumsum_rows(l_incl, x):
    lb = l_incl.astype(BF16)
    return sum(jnp.dot(lb, xp, preferred_element_type=F32) for xp in _split(x, 2))


def _head_sum(x, ones_bd):
    return jnp.dot(x.astype(BF16), ones_bd.astype(BF16), preferred_element_type=F32)


def _rmsnorm(x, w, eps=NORM_EPS):
    return x * lax.rsqrt(jnp.mean(x * x, axis=-1, keepdims=True) + eps) * w


def _sigmoid(x):
    return 1.0 / (1.0 + jnp.exp(-x))


def _silu(x):
    return x * _sigmoid(x)


def _softplus(x):
    return jnp.maximum(x, 0.0) + jnp.log(1.0 + jnp.exp(-jnp.abs(x)))


def _sqrt(x):
    return jnp.where(x > 0.0, x * lax.rsqrt(x), 0.0)


def _gelu_tanh(x):
    return 0.5 * x * (1.0 + jnp.tanh(0.7978845608028654 * (x + 0.044715 * x * x * x)))


def _iota(shape, dim):
    return lax.broadcasted_iota(jnp.int32, shape, dim)


def _shift_rows(x, x_prev, k, row):
    return jnp.where(row >= k, pltpu.roll(x, k, 0), pltpu.roll(x_prev, k, 0))


def _causal_conv(x, x_prev, w_ref, row):
    y = w_ref[CONV_WIDTH - 1:CONV_WIDTH, :] * x
    for k in range(1, CONV_WIDTH):
        y = y + w_ref[CONV_WIDTH - 1 - k:CONV_WIDTH - k, :] * _shift_rows(x, x_prev, k, row)
    return y


def _unit_lower_inverse(a):
    inv = jnp.where(_iota(a.shape, 0) == _iota(a.shape, 1), 1.0, 0.0) - a
    p = a
    for _ in range(INV_LEVELS - 1):
        p = _dot_hi(p, p)
        inv = inv + _dot_hi(inv, p)
    return inv


def _unit_lower_inverse_many(mats):
    eye = jnp.where(_iota(mats[0].shape, 0) == _iota(mats[0].shape, 1), 1.0, 0.0)
    inv = [eye - a for a in mats]
    p = mats
    for _ in range(INV_LEVELS - 1):
        p = [_dot_hi(x, x) for x in p]
        inv = [i + _dot_hi(i, x) for i, x in zip(inv, p)]
    return inv


def _linear_scan(a, b, row):
    k = 1
    while k < a.shape[0]:
        a_s = jnp.where(row >= k, pltpu.roll(a, k, 0), 1.0)
        b_s = jnp.where(row >= k, pltpu.roll(b, k, 0), 0.0)
        b = a * b_s + b
        a = a * a_s
        k *= 2
    return a, b


def _resident(block_shape, index):
    return pl.BlockSpec(block_shape, lambda i: index, pipeline_mode=pl.Buffered(1))


def _ffn_kernel(*refs, f_chunk, final_norm, fuse_proj):
    if fuse_proj:
        x_ref, y_ref, wo_ref, nw_ref, wg_ref, wu_ref, wd_ref, fw_ref, o_ref = refs
        x = x_ref[...] + _dot(y_ref[...].astype(BF16), wo_ref[...].astype(BF16))
    else:
        x_ref, nw_ref, wg_ref, wu_ref, wd_ref, fw_ref, o_ref = refs
        x = x_ref[...]
    h = _rmsnorm(x, nw_ref[...]).astype(BF16)
    acc = jnp.zeros_like(x)
    for j in range(D_FF // f_chunk):
        lo = j * f_chunk
        g = _dot(h, wg_ref[:, lo:lo + f_chunk].astype(BF16))
        u = _dot(h, wu_ref[:, lo:lo + f_chunk].astype(BF16))
        acc = acc + _dot((_silu(g) * u).astype(BF16), wd_ref[lo:lo + f_chunk, :].astype(BF16))
    y = x + 0.5 * acc
    if final_norm:
        y = _rmsnorm(y, fw_ref[...])
    o_ref[...] = y


def _ffn(x, norm_w, w_gate, w_up, w_down, lj, final_w, *, final_norm, proj=None,
         tm=512, f_chunk=256):
    t, d = x.shape
    rows = lambda i: (i, 0)
    row_spec = pl.BlockSpec((tm, d), rows)
    vec_spec = pl.BlockSpec((1, d), lambda i: (0, 0))
    in_specs, args = [row_spec], [x]
    if proj is not None:
        y, w_out, i_out = proj
        k = y.shape[1]
        in_specs += [pl.BlockSpec((tm, k), rows), _resident((None, k, d), (i_out, 0, 0))]
        args += [y, w_out]
    in_specs += [vec_spec,
                 _resident((None, None, d, D_FF), lj + (0, 0)),
                 _resident((None, None, d, D_FF), lj + (0, 0)),
                 _resident((None, None, D_FF, d), lj + (0, 0)),
                 vec_spec]
    args += [norm_w.reshape(1, d), w_gate, w_up, w_down, final_w.reshape(1, d)]
    return pl.pallas_call(
        functools.partial(_ffn_kernel, f_chunk=f_chunk, final_norm=final_norm,
                          fuse_proj=proj is not None),
        grid=(t // tm,),
        in_specs=in_specs,
        out_specs=row_spec,
        out_shape=jax.ShapeDtypeStruct((t, d), F32),
        compiler_params=pltpu.CompilerParams(
            dimension_semantics=("arbitrary",), vmem_limit_bytes=VMEM_LIMIT),
        name="ffn_proj" if proj is not None else "ffn",
    )(*args)


def _norm_proj_kernel(x_ref, nw_ref, w_ref, o_ref):
    h = _rmsnorm(x_ref[...], nw_ref[...]).astype(BF16)
    o_ref[...] = _dot(h, w_ref[...].astype(BF16))


def _norm_proj(x, norm_w, w, i_w, *, tm=512):
    t, d = x.shape
    n = w.shape[2]
    return pl.pallas_call(
        _norm_proj_kernel,
        grid=(t // tm,),
        in_specs=[
            pl.BlockSpec((tm, d), lambda i: (i, 0)),
            pl.BlockSpec((1, d), lambda i: (0, 0)),
            _resident((None, d, n), (i_w, 0, 0)),
        ],
        out_specs=pl.BlockSpec((tm, n), lambda i: (i, 0)),
        out_shape=jax.ShapeDtypeStruct((t, n), F32),
        compiler_params=pltpu.CompilerParams(
            dimension_semantics=("arbitrary",), vmem_limit_bytes=VMEM_LIMIT),
        name="norm_proj",
    )(x, norm_w.reshape(1, d), w)


def _stack(x, m0, m1):
    return jnp.concatenate([x * m0, x * m1], axis=0)


def _unstack(y):
    return y[:CHUNK] + y[CHUNK:]


def _mixer_ab_kernel(z_ref, cw_ref, cb_ref, wa_ref, ba_ref, wx_ref, bx_ref, ap_ref, mu_ref,
                     w0_ref, w2_ref, a0_ref, a2_ref, g2_ref, kk_ref, ka_ref, rk_ref,
                     lnw_ref, lnb_ref, o_ref, zprev_ref, hlru_ref, state_ref):
    n_batch = z_ref.shape[0]
    c = CHUNK
    L = LRU_WIDTH

    @pl.when(pl.program_id(0) == 0)
    def _():
        zprev_ref[...] = jnp.zeros_like(zprev_ref)
        hlru_ref[...] = jnp.zeros_like(hlru_ref)
        state_ref[...] = jnp.zeros_like(state_ref)

    row = _iota((c, 1), 0)
    lane = _iota((1, LANES), 1)
    m0 = jnp.where(lane < RWKV_HEAD_DIM, 1.0, 0.0)
    m1 = 1.0 - m0
    ri = _iota((c, c), 0)
    ci = _iota((c, c), 1)
    l_incl = jnp.where(ri >= ci, 1.0, 0.0)
    si = _iota((LANES, LANES), 0)
    sj = _iota((LANES, LANES), 1)
    same_head = (si // RWKV_HEAD_DIM) == (sj // RWKV_HEAD_DIM)
    ones_bd = jnp.where(same_head, 1.0, 0.0)
    strict_bd = same_head & (si > sj)
    incl_bd = same_head & (si >= sj)
    inv_n = 1.0 / RWKV_HEAD_DIM

    nb = range(n_batch)
    groups = [(b, p) for b in nb for p in range(N_PAIRS)]
    tile = lambda x, p: x[:, p * LANES:(p + 1) * LANES]
    z = [z_ref[b] for b in nb]
    zp = [zprev_ref[b] for b in nb]

    xc = [_causal_conv(z[b][:, :L], zp[b][:, :L], cw_ref, row) + cb_ref[...] for b in nb]
    xcb = [x.astype(BF16) for x in xc]
    sp_a = _softplus(-ap_ref[...])
    r_gate = [_sigmoid(_dot(tile(xcb[b], p), wa_ref[p]) + tile(ba_ref, p)) for b, p in groups]
    i_gate = [_sigmoid(_dot(tile(xcb[b], p), wx_ref[p]) + tile(bx_ref, p)) for b, p in groups]
    log_a = [-LRU_C * rg_ * tile(sp_a, p) for rg_, (b, p) in zip(r_gate, groups)]
    a = [jnp.exp(x) for x in log_a]
    u = [_sqrt(jnp.tanh(-la) * (a_ * a_ + 1.0)) * (ig * tile(xc[b], p))
         for la, a_, ig, (b, p) in zip(log_a, a, i_gate, groups)]
    scans = [_linear_scan(a_, u_, row) for a_, u_ in zip(a, u)]
    for (a_cum, h), (b, p) in zip(scans, groups):
        sl = slice(p * LANES, (p + 1) * LANES)
        h = h + a_cum * hlru_ref[b, 0:1, sl]
        hlru_ref[b, 0:1, sl] = h[c - 1:c, :]
        o_ref[b, :, sl] = h * _gelu_tanh(z[b][:, L + p * LANES:L + (p + 1) * LANES])

    R = RWKV_WIDTH
    s = [x[:, 2 * L:] for x in z]
    pmix = [s[b] + mu_ref[...] * (_shift_rows(s[b], zp[b][:, 2 * L:], 1, row) - s[b]) for b in nb]
    r = [x[:, :R] for x in pmix]
    k = [x[:, R:2 * R] for x in pmix]
    v = [x[:, 2 * R:3 * R] for x in pmix]
    wa_in = [x[:, 3 * R:3 * R + LANES] for x in pmix]
    gl_in = [x[:, 3 * R + LANES:] for x in pmix]
    w_log = [-_softplus(-(w0_ref[...] + _dot(jnp.tanh(x).astype(BF16), w2_ref[...]))) - 0.5
             for x in wa_in]
    lw = [-jnp.exp(x) for x in w_log]
    a_lr = [_sigmoid(a0_ref[...] + _dot(x.astype(BF16), a2_ref[...])) for x in wa_in]
    g_out = [_dot(_sigmoid(x).astype(BF16), g2_ref[...]) for x in gl_in]
    kkr = [x * kk_ref[...] for x in k]
    k_mod = [kx * (1.0 + (al - 1.0) * ka_ref[...]) for kx, al in zip(k, a_lr)]
    cs = [_cumsum_rows(l_incl, x) for x in lw]
    cl = [x[c - 1:c, :] for x in cs]
    g_in = [jnp.exp(x) for x in cs]
    g_ex = [jnp.exp(x - y) for x, y in zip(cs, lw)]
    g_inv = [jnp.exp(-x) for x in cs]
    g_end = [jnp.exp(x - y) for x, y in zip(cl, cs)]
    g_tot = [jnp.exp(x) for x in cl]

    G = lambda xs: [tile(xs[b], p) for b, p in groups]
    rp, vp, ktp, kn, alp = G(r), G(v), G(k_mod), G(kkr), G(a_lr)
    gin, gex, ginv, gend, gtot = G(g_in), G(g_ex), G(g_inv), G(g_end), G(g_tot)
    n2 = [_head_sum(x * x, ones_bd) for x in kn]
    kkp = [x / jnp.maximum(jnp.sqrt(n), 1e-12) for x, n in zip(kn, n2)]
    bp = [x * y for x, y in zip(kkp, alp)]
    stk = lambda x: _stack(x, m0, m1)
    kg_st = [stk(x * y) for x, y in zip(kkp, gex)]
    rg = [x * y for x, y in zip(rp, gin)]
    bi_st = [stk(x * y) for x, y in zip(bp, ginv)]
    ki_st = [stk(x * y) for x, y in zip(ktp, ginv)]
    kd_st = [stk(x * y) for x, y in zip(ktp, gend)]
    bd_st = [stk(x * y) for x, y in zip(bp, gend)]
    v_st = [stk(x) for x in vp]
    m = [_dot_nt(jnp.concatenate([kg, stk(rg_)], axis=0), jnp.concatenate([bi, ki], axis=0))
         for kg, rg_, bi, ki in zip(kg_st, rg, bi_st, ki_st)]
    a_m = [jnp.where(strict_bd, x[:2 * c, :2 * c], 0.0) for x in m]
    b_m = [jnp.where(strict_bd, x[:2 * c, 2 * c:], 0.0) for x in m]
    q_m = [jnp.where(incl_bd, x[2 * c:, :2 * c], 0.0) for x in m]
    p_m = [jnp.where(incl_bd, x[2 * c:, 2 * c:], 0.0) for x in m]
    t_inv = _unit_lower_inverse_many(a_m)
    bv = [_dot_hi(x, y) for x, y in zip(b_m, v_st)]
    wu = [_dot_hi(t, jnp.concatenate([kg, x], axis=1)) for t, kg, x in zip(t_inv, kg_st, bv)]
    pv = [_dot_hi(x, y) for x, y in zip(p_m, v_st)]
    vkd = [_dot_tn(x, y) for x, y in zip(v_st, kd_st)]
    st = [state_ref[b, p] for b, p in groups]
    u_st = [_dot_nt(x[:, :LANES], s_) + x[:, LANES:] for x, s_ in zip(wu, st)]
    rs = [_dot_nt(x, s_) for x, s_ in zip(rg, st)]
    qu = [_dot_hi(x, y) for x, y in zip(q_m, u_st)]
    ubd = [_dot_tn(x, y) for x, y in zip(u_st, bd_st)]
    y = [x + _unstack(p_ - q_) for x, p_, q_ in zip(rs, pv, qu)]
    for i, (b, p) in enumerate(groups):
        state_ref[b, p] = st[i] * gtot[i] + vkd[i] - ubd[i]

    mean = [_head_sum(x, ones_bd) * inv_n for x in y]
    d = [x - mu_ for x, mu_ in zip(y, mean)]
    var = [_head_sum(x * x, ones_bd) * inv_n for x in d]
    bonus = [_head_sum(r_ * kt * tile(rk_ref, p), ones_bd) * v_
             for r_, kt, v_, (b, p) in zip(rp, ktp, vp, groups)]
    for i, (b, p) in enumerate(groups):
        yn = d[i] * lax.rsqrt(var[i] + RWKV_GN_EPS) * tile(lnw_ref, p) + tile(lnb_ref, p)
        o_ref[b, :, L + p * LANES:L + (p + 1) * LANES] = (yn + bonus[i]) * tile(g_out[b], p)
    for b in nb:
        zprev_ref[b] = z[b]


def _pair_blockdiag(w):
    g = w.reshape(N_PAIRS, PAIR, LRU_BLOCK, LRU_BLOCK)
    z = jnp.zeros((N_PAIRS, LRU_BLOCK, LRU_BLOCK), w.dtype)
    top = jnp.concatenate([g[:, 0], z], axis=2)
    bot = jnp.concatenate([z, g[:, 1]], axis=2)
    return jnp.concatenate([top, bot], axis=1)


def _mixer_ab(z, conv_w, conv_b, ga_w, ga_b, gx_w, gx_b, a_param, mu, w0, w2, a0, a2, g2,
              k_k, k_a, r_k, ln_w, ln_b):
    b, s, n = z.shape
    row = lambda t: t.reshape(1, -1)
    w2p = jnp.concatenate([w2, jnp.zeros((ICLR_RANK, RWKV_WIDTH), F32)], axis=0).astype(BF16)
    a2p = jnp.concatenate([jnp.zeros((DECAY_RANK, RWKV_WIDTH), F32), a2], axis=0).astype(BF16)
    args = (z, conv_w, row(conv_b), _pair_blockdiag(ga_w).astype(BF16), row(ga_b),
            _pair_blockdiag(gx_w).astype(BF16), row(gx_b), row(a_param), row(mu), row(w0), w2p,
            row(a0), a2p, g2.astype(BF16), row(k_k), row(k_a), row(r_k), row(ln_w), row(ln_b))

    def full(a):
        nd = a.ndim
        return pl.BlockSpec(a.shape, lambda i: (0,) * nd)

    return pl.pallas_call(
        _mixer_ab_kernel,
        grid=(s // CHUNK,),
        in_specs=[pl.BlockSpec((b, CHUNK, n), lambda i: (0, i, 0))] + [full(a) for a in args[1:]],
        out_specs=pl.BlockSpec((b, CHUNK, D_MODEL), lambda i: (0, i, 0)),
        out_shape=jax.ShapeDtypeStruct((b, s, D_MODEL), F32),
        scratch_shapes=[
            pltpu.VMEM((b, CHUNK, n), F32),
            pltpu.VMEM((b, 8, LRU_WIDTH), F32),
            pltpu.VMEM((b, N_PAIRS, LANES, LANES), F32),
        ],
        compiler_params=pltpu.CompilerParams(
            dimension_semantics=("arbitrary",), vmem_limit_bytes=VMEM_LIMIT),
        name="mixer_ab",
    )(*args)


def _mixer_c_kernel(z_ref, cw_ref, alog_ref, dt_ref, nw_ref, o_ref, zprev_ref, state_ref):
    n_batch = z_ref.shape[0]
    c = CHUNK
    W = DN_WIDTH
    dh = DN_HEAD_DIM

    @pl.when(pl.program_id(0) == 0)
    def _():
        zprev_ref[...] = jnp.zeros_like(zprev_ref)
        state_ref[...] = jnp.zeros_like(state_ref)

    row = _iota((c, 1), 0)
    ri = _iota((c, c), 0)
    ci = _iota((c, c), 1)
    causal = ri >= ci
    strict = ri > ci
    l_incl = jnp.where(causal, 1.0, 0.0)

    groups = [(b, h) for b in range(n_batch) for h in range(DN_HEADS)]
    zq = [z_ref[b, :, :3 * W] for b in range(n_batch)]
    qkv = [_silu(_causal_conv(zq[b], zprev_ref[b], cw_ref, row)) for b in range(n_batch)]
    zs = [z_ref[b, :, 4 * W:] for b in range(n_batch)]
    beta_all = [_sigmoid(x) for x in zs]
    g_all = [-jnp.exp(alog_ref[...]) * _softplus(x + dt_ref[...]) for x in zs]
    gc_all = [_cumsum_rows(l_incl, x) for x in g_all]
    gc_rows = [x.T for x in gc_all]

    beta = [beta_all[b][:, h:h + 1] for b, h in groups]
    gcol = [gc_all[b][:, DN_HEADS + h:DN_HEADS + h + 1] for b, h in groups]
    grow = [gc_rows[b][DN_HEADS + h:DN_HEADS + h + 1, :] for b, h in groups]
    glast = [x[c - 1:c, :] for x in gcol]
    q = [qkv[b][:, h * dh:(h + 1) * dh] for b, h in groups]
    k = [qkv[b][:, W + h * dh:W + (h + 1) * dh] for b, h in groups]
    v = [qkv[b][:, 2 * W + h * dh:2 * W + (h + 1) * dh] for b, h in groups]
    q = [x * lax.rsqrt(jnp.sum(x * x, axis=-1, keepdims=True) + 1e-6) * (dh ** -0.5) for x in q]
    k = [x * lax.rsqrt(jnp.sum(x * x, axis=-1, keepdims=True) + 1e-6) for x in k]
    decay = [jnp.exp(jnp.where(causal, gc - gr, -jnp.inf)) for gc, gr in zip(gcol, grow)]
    kk = [_dot_nt(x, x) for x in k]
    a_m = [jnp.where(strict, bt * m * d, 0.0) for bt, m, d in zip(beta, kk, decay)]
    t_inv = _unit_lower_inverse_many(a_m)
    egc = [jnp.exp(x) for x in gcol]
    rhs = [jnp.concatenate([vv * bt, kx * (bt * e)], axis=1)
           for vv, kx, bt, e in zip(v, k, beta, egc)]
    sol = [_dot_hi(t, r) for t, r in zip(t_inv, rhs)]
    attn = [_dot_nt(qx, kx) * d for qx, kx, d in zip(q, k, decay)]
    st = [state_ref[b, h] for b, h in groups]
    ws = [_dot_hi(x[:, dh:], s_) for x, s_ in zip(sol, st)]
    v_new = [x[:, :dh] - y for x, y in zip(sol, ws)]
    qs = [_dot_hi(qx * e, s_) for qx, e, s_ in zip(q, egc, st)]
    av = [_dot_hi(a, vn) for a, vn in zip(attn, v_new)]
    kv = [_dot_tn(kx * jnp.exp(gl - gc), vn) for kx, gl, gc, vn in zip(k, glast, gcol, v_new)]
    for i, (b, h) in enumerate(groups):
        state_ref[b, h] = st[i] * jnp.exp(glast[i]) + kv[i]
        gate = z_ref[b, :, 3 * W + h * dh:3 * W + (h + 1) * dh]
        o_ref[b, :, h * dh:(h + 1) * dh] = _rmsnorm(qs[i] + av[i], nw_ref[...]) * _silu(gate)
    for b in range(n_batch):
        zprev_ref[b] = zq[b]


def _mixer_c(z, conv_w, a_log, dt_bias, norm_w):
    b, s, n = z.shape
    pad = jnp.zeros((C_SMALL - 2 * DN_HEADS,), F32)
    lead = jnp.zeros((DN_HEADS,), F32)
    alog_row = jnp.concatenate([lead, a_log, pad]).reshape(1, C_SMALL)
    dt_row = jnp.concatenate([lead, dt_bias, pad]).reshape(1, C_SMALL)
    args = (z, conv_w, alog_row, dt_row, norm_w.reshape(1, DN_HEAD_DIM))

    def full(a):
        nd = a.ndim
        return pl.BlockSpec(a.shape, lambda i: (0,) * nd)

    return pl.pallas_call(
        _mixer_c_kernel,
        grid=(s // CHUNK,),
        in_specs=[pl.BlockSpec((b, CHUNK, n), lambda i: (0, i, 0))] + [full(a) for a in args[1:]],
        out_specs=pl.BlockSpec((b, CHUNK, DN_WIDTH), lambda i: (0, i, 0)),
        out_shape=jax.ShapeDtypeStruct((b, s, DN_WIDTH), F32),
        scratch_shapes=[
            pltpu.VMEM((b, CHUNK, 3 * DN_WIDTH), F32),
            pltpu.VMEM((b, DN_HEADS, DN_HEAD_DIM, DN_HEAD_DIM), F32),
        ],
        compiler_params=pltpu.CompilerParams(
            dimension_semantics=("arbitrary",), vmem_limit_bytes=VMEM_LIMIT),
        name="mixer_c",
    )(*args)


def kernel(x, ffn_norm, ffn_w_gate, ffn_w_up, ffn_w_down, mix_norm, final_norm, ab_w_in, lru_conv_w, lru_conv_b, lru_gate_a_w, lru_gate_a_b, lru_gate_x_w, lru_gate_x_b, lru_a_param, rwkv_mu, rwkv_w0, rwkv_w2, rwkv_a0, rwkv_a2, rwkv_g2, rwkv_k_k, rwkv_k_a, rwkv_r_k, rwkv_ln_w, rwkv_ln_b, ab_w_out, c_w_in, dn_conv_w, dn_A_log, dn_dt_bias, dn_norm, c_w_out):
    b, s, d = x.shape
    depth = ffn_norm.shape[0]
    ffn_w = (ffn_w_gate, ffn_w_up, ffn_w_down)
    xf = x.reshape(b * s, d)
    for l in range(depth):
        i = l // 2
        xf = _ffn(xf, ffn_norm[l, 0], *ffn_w, (l, 0), final_norm, final_norm=False)
        if l % 2 == 0:
            z = _norm_proj(xf, mix_norm[l], ab_w_in, i).reshape(b, s, AB_IN)
            y = _mixer_ab(z, lru_conv_w[i], lru_conv_b[i], lru_gate_a_w[i], lru_gate_a_b[i],
                          lru_gate_x_w[i], lru_gate_x_b[i], lru_a_param[i], rwkv_mu[i],
                          rwkv_w0[i], rwkv_w2[i], rwkv_a0[i], rwkv_a2[i], rwkv_g2[i],
                          rwkv_k_k[i], rwkv_k_a[i], rwkv_r_k[i], rwkv_ln_w[i], rwkv_ln_b[i])
            proj = (y.reshape(b * s, D_MODEL), ab_w_out, i)
        else:
            w_in = jnp.pad(c_w_in, ((0, 0), (0, 0), (0, C_IN_PAD - c_w_in.shape[2])))
            z = _norm_proj(xf, mix_norm[l], w_in, i).reshape(b, s, C_IN_PAD)
            y = _mixer_c(z, dn_conv_w[i], dn_A_log[i], dn_dt_bias[i], dn_norm[i])
            proj = (y.reshape(b * s, DN_WIDTH), c_w_out, i)
        xf = _ffn(xf, ffn_norm[l, 1], *ffn_w, (l, 1), final_norm,
                  final_norm=(l == depth - 1), proj=proj)
    return xf.reshape(b, s, d)
```

```python
import functools

import jax
import jax.numpy as jnp
from jax import lax
from jax.experimental import pallas as pl
from jax.experimental.pallas import tpu as pltpu

F32 = jnp.float32
BF16 = jnp.bfloat16

D_MODEL = 1024
D_FF = 2816
NORM_EPS = 1e-6
CONV_WIDTH = 4

LRU_WIDTH = 512
LRU_BLOCK = 64
LRU_C = 8.0

RWKV_WIDTH = 512
RWKV_HEAD_DIM = 64
DECAY_RANK = 64
ICLR_RANK = 64
GATE_RANK = 128
RWKV_GN_EPS = 64e-5
SHIFT_WIDTH = 3 * RWKV_WIDTH + DECAY_RANK + ICLR_RANK + GATE_RANK
AB_IN = 2 * LRU_WIDTH + SHIFT_WIDTH

DN_HEADS = 8
DN_HEAD_DIM = 128
DN_WIDTH = DN_HEADS * DN_HEAD_DIM
C_SMALL = 128
C_IN_PAD = 4 * DN_WIDTH + C_SMALL

LANES = 128
CHUNK = 64
CHUNKS_PER_STEP = 2
TAIL = 8
PAIR = LANES // RWKV_HEAD_DIM
N_PAIRS = RWKV_WIDTH // LANES
INV_LEVELS = 6
VMEM_LIMIT = 56 * 1024 * 1024


def _dot(a, b):
    return jnp.dot(a, b, preferred_element_type=F32)


def _rec_dot(a, b, contract):
    return lax.dot_general(a.astype(BF16), b.astype(BF16), (contract, ((), ())),
                           preferred_element_type=F32)


def _dot_hi(a, b):
    return _rec_dot(a, b, ((1,), (0,)))


def _dot_nt(a, b):
    return _rec_dot(a, b, ((1,), (1,)))


def _dot_tn(a, b):
    return _rec_dot(a, b, ((0,), (0,)))


def _cumsum_rows(l_incl, x):
    lb = l_incl.astype(BF16)
    hi = x.astype(BF16)
    lo = (x - hi.astype(F32)).astype(BF16)
    return _dot(lb, hi) + _dot(lb, lo)


def _head_sum(x, ones_bd):
    return jnp.dot(x.astype(BF16), ones_bd.astype(BF16), preferred_element_type=F32)


def _rmsnorm(x, w, eps=NORM_EPS):
    return x * lax.rsqrt(jnp.mean(x * x, axis=-1, keepdims=True) + eps) * w


def _sigmoid(x):
    return 0.5 + 0.5 * jnp.tanh(0.5 * x)


def _silu(x):
    h = 0.5 * x
    return h + h * jnp.tanh(h)


def _softplus(x):
    return jnp.maximum(x, 0.0) + jnp.log(1.0 + jnp.exp(-jnp.abs(x)))


def _sqrt(x):
    return jnp.where(x > 0.0, x * lax.rsqrt(x), 0.0)


def _gelu_tanh(x):
    return 0.5 * x * (1.0 + jnp.tanh(0.7978845608028654 * (x + 0.044715 * x * x * x)))


def _iota(shape, dim):
    return lax.broadcasted_iota(jnp.int32, shape, dim)


def _with_tail(x, tail):
    return jnp.concatenate([tail, x], axis=0)


def _shift_rows(xt, k):
    return pltpu.roll(xt, k, 0)[TAIL:]


def _causal_conv(x, tail, w_ref):
    xt = _with_tail(x, tail)
    y = w_ref[CONV_WIDTH - 1:CONV_WIDTH, :] * x
    for k in range(1, CONV_WIDTH):
        y = y + w_ref[CONV_WIDTH - 1 - k:CONV_WIDTH - k, :] * _shift_rows(xt, k)
    return y


def _unit_lower_inverse_many(mats):
    eye = jnp.where(_iota(mats[0].shape, 0) == _iota(mats[0].shape, 1), 1.0, 0.0)
    inv = [eye - a for a in mats]
    p = mats
    for _ in range(INV_LEVELS - 1):
        p = [_dot_hi(x, x) for x in p]
        inv = [i + _dot_hi(i, x) for i, x in zip(inv, p)]
    return inv


def _linear_scan(a, b, row):
    k = 1
    while k < a.shape[0]:
        a_s = jnp.where(row >= k, pltpu.roll(a, k, 0), 1.0)
        b_s = jnp.where(row >= k, pltpu.roll(b, k, 0), 0.0)
        b = a * b_s + b
        a = a * a_s
        k *= 2
    return a, b


def _resident(block_shape, index):
    return pl.BlockSpec(block_shape, lambda i: index, pipeline_mode=pl.Buffered(1))


def _ffn_kernel(*refs, f_chunk, final_norm, fuse_proj):
    if fuse_proj:
        x_ref, y_ref, wo_ref, nw_ref, wg_ref, wu_ref, wd_ref, fw_ref, o_ref = refs
        x = x_ref[...] + _dot(y_ref[...].astype(BF16), wo_ref[...].astype(BF16))
    else:
        x_ref, nw_ref, wg_ref, wu_ref, wd_ref, fw_ref, o_ref = refs
        x = x_ref[...]
    h = _rmsnorm(x, nw_ref[...]).astype(BF16)
    acc = jnp.zeros_like(x)
    for j in range(D_FF // f_chunk):
        lo = j * f_chunk
        g = _dot(h, wg_ref[:, lo:lo + f_chunk].astype(BF16))
        u = _dot(h, wu_ref[:, lo:lo + f_chunk].astype(BF16))
        acc = acc + _dot((_silu(g) * u).astype(BF16), wd_ref[lo:lo + f_chunk, :].astype(BF16))
    y = x + 0.5 * acc
    if final_norm:
        y = _rmsnorm(y, fw_ref[...])
    o_ref[...] = y


def _ffn(x, norm_w, w_gate, w_up, w_down, lj, final_w, *, final_norm, proj=None,
         tm=512, f_chunk=256):
    t, d = x.shape
    rows = lambda i: (i, 0)
    row_spec = pl.BlockSpec((tm, d), rows)
    vec_spec = pl.BlockSpec((1, d), lambda i: (0, 0))
    in_specs, args = [row_spec], [x]
    if proj is not None:
        y, w_out, i_out = proj
        k = y.shape[1]
        in_specs += [pl.BlockSpec((tm, k), rows), _resident((None, k, d), (i_out, 0, 0))]
        args += [y, w_out]
    in_specs += [vec_spec,
                 _resident((None, None, d, D_FF), lj + (0, 0)),
                 _resident((None, None, d, D_FF), lj + (0, 0)),
                 _resident((None, None, D_FF, d), lj + (0, 0)),
                 vec_spec]
    args += [norm_w.reshape(1, d), w_gate, w_up, w_down, final_w.reshape(1, d)]
    return pl.pallas_call(
        functools.partial(_ffn_kernel, f_chunk=f_chunk, final_norm=final_norm,
                          fuse_proj=proj is not None),
        grid=(t // tm,),
        in_specs=in_specs,
        out_specs=row_spec,
        out_shape=jax.ShapeDtypeStruct((t, d), F32),
        compiler_params=pltpu.CompilerParams(
            dimension_semantics=("arbitrary",), vmem_limit_bytes=VMEM_LIMIT),
        name="ffn_proj" if proj is not None else "ffn",
    )(*args)


def _norm_proj_kernel(x_ref, nw_ref, w_ref, o_ref):
    h = _rmsnorm(x_ref[...], nw_ref[...]).astype(BF16)
    o_ref[...] = _dot(h, w_ref[...].astype(BF16))


def _norm_proj(x, norm_w, w, i_w, *, tm=512):
    t, d = x.shape
    n = w.shape[2]
    return pl.pallas_call(
        _norm_proj_kernel,
        grid=(t // tm,),
        in_specs=[
            pl.BlockSpec((tm, d), lambda i: (i, 0)),
            pl.BlockSpec((1, d), lambda i: (0, 0)),
            _resident((None, d, n), (i_w, 0, 0)),
        ],
        out_specs=pl.BlockSpec((tm, n), lambda i: (i, 0)),
        out_shape=jax.ShapeDtypeStruct((t, n), F32),
        compiler_params=pltpu.CompilerParams(
            dimension_semantics=("arbitrary",), vmem_limit_bytes=VMEM_LIMIT),
        name="norm_proj",
    )(x, norm_w.reshape(1, d), w)


def _stack(x, m0, m1):
    return jnp.concatenate([x * m0, x * m1], axis=0)


def _unstack(y):
    return y[:CHUNK] + y[CHUNK:]


def _mixer_ab_kernel(z_ref, cw_ref, cb_ref, wa_ref, ba_ref, wx_ref, bx_ref, ap_ref, mu_ref,
                     w0_ref, w2_ref, a0_ref, a2_ref, g2_ref, kk_ref, ka_ref, rk_ref,
                     lnw_ref, lnb_ref, o_ref, zprev_ref, hlru_ref, state_ref):
    n_batch = z_ref.shape[0]
    c = CHUNK
    L = LRU_WIDTH

    @pl.when(pl.program_id(0) == 0)
    def _():
        zprev_ref[...] = jnp.zeros_like(zprev_ref)
        hlru_ref[...] = jnp.zeros_like(hlru_ref)
        state_ref[...] = jnp.zeros_like(state_ref)

    row = _iota((c, 1), 0)
    lane = _iota((1, LANES), 1)
    m0 = jnp.where(lane < RWKV_HEAD_DIM, 1.0, 0.0)
    m1 = 1.0 - m0
    ri = _iota((c, c), 0)
    ci = _iota((c, c), 1)
    l_incl = jnp.where(ri >= ci, 1.0, 0.0)
    si = _iota((LANES, LANES), 0)
    sj = _iota((LANES, LANES), 1)
    same_head = (si // RWKV_HEAD_DIM) == (sj // RWKV_HEAD_DIM)
    ones_bd = jnp.where(same_head, 1.0, 0.0)
    strict_bd = same_head & (si > sj)
    incl_bd = same_head & (si >= sj)
    inv_n = 1.0 / RWKV_HEAD_DIM

    n_sub = z_ref.shape[1] // c
    units = [(j, b) for j in range(n_sub) for b in range(n_batch)]
    nu = range(len(units))
    groups = [(u, p) for u in nu for p in range(N_PAIRS)]
    tile = lambda x, p: x[:, p * LANES:(p + 1) * LANES]
    rows = lambda j: slice(j * c, (j + 1) * c)
    z = [z_ref[b, rows(j), :] for j, b in units]
    zt = [zprev_ref[b] if j == 0 else z_ref[b, j * c - TAIL:j * c, :] for j, b in units]

    xc = [_causal_conv(z[u][:, :L], zt[u][:, :L], cw_ref) + cb_ref[...] for u in nu]
    xcb = [x.astype(BF16) for x in xc]
    sp_a = _softplus(-ap_ref[...])
    r_gate = [_sigmoid(_dot(tile(xcb[u], p), wa_ref[p]) + tile(ba_ref, p)) for u, p in groups]
    i_gate = [_sigmoid(_dot(tile(xcb[u], p), wx_ref[p]) + tile(bx_ref, p)) for u, p in groups]
    log_a = [-LRU_C * rg_ * tile(sp_a, p) for rg_, (u, p) in zip(r_gate, groups)]
    a = [jnp.exp(x) for x in log_a]
    u_in = [_sqrt(jnp.tanh(-la) * (a_ * a_ + 1.0)) * (ig * tile(xc[u], p))
            for la, a_, ig, (u, p) in zip(log_a, a, i_gate, groups)]
    scans = [_linear_scan(a_, u_, row) for a_, u_ in zip(a, u_in)]
    gelu = [_gelu_tanh(z[u][:, L + p * LANES:L + (p + 1) * LANES]) for u, p in groups]
    for (a_cum, h), gl_, (u, p) in zip(scans, gelu, groups):
        j, b = units[u]
        sl = slice(p * LANES, (p + 1) * LANES)
        h = h + a_cum * hlru_ref[b, 0:1, sl]
        hlru_ref[b, 0:1, sl] = h[c - 1:c, :]
        o_ref[b, rows(j), sl] = h * gl_

    R = RWKV_WIDTH
    s = [x[:, 2 * L:] for x in z]
    pmix = [s[u] + mu_ref[...] * (_shift_rows(_with_tail(s[u], zt[u][:, 2 * L:]), 1) - s[u])
            for u in nu]
    r = [x[:, :R] for x in pmix]
    k = [x[:, R:2 * R] for x in pmix]
    v = [x[:, 2 * R:3 * R] for x in pmix]
    wa_in = [x[:, 3 * R:3 * R + LANES] for x in pmix]
    gl_in = [x[:, 3 * R + LANES:] for x in pmix]
    w_log = [-_softplus(-(w0_ref[...] + _dot(jnp.tanh(x).astype(BF16), w2_ref[...]))) - 0.5
             for x in wa_in]
    lw = [-jnp.exp(x) for x in w_log]
    a_lr = [_sigmoid(a0_ref[...] + _dot(x.astype(BF16), a2_ref[...])) for x in wa_in]
    g_out = [_dot(_sigmoid(x).astype(BF16), g2_ref[...]) for x in gl_in]
    kkr = [x * kk_ref[...] for x in k]
    k_mod = [kx * (1.0 + (al - 1.0) * ka_ref[...]) for kx, al in zip(k, a_lr)]
    cs = [_cumsum_rows(l_incl, x) for x in lw]
    cl = [x[c - 1:c, :] for x in cs]
    g_in = [jnp.exp(x) for x in cs]
    g_ex = [jnp.exp(x - y) for x, y in zip(cs, lw)]
    g_inv = [jnp.exp(-x) for x in cs]
    g_end = [jnp.exp(x - y) for x, y in zip(cl, cs)]
    g_tot = [jnp.exp(x) for x in cl]

    G = lambda xs: [tile(xs[u], p) for u, p in groups]
    rp, vp, ktp, kn, alp = G(r), G(v), G(k_mod), G(kkr), G(a_lr)
    gin, gex, ginv, gend, gtot = G(g_in), G(g_ex), G(g_inv), G(g_end), G(g_tot)
    n2 = [_head_sum(x * x, ones_bd) for x in kn]
    kkp = [x / jnp.maximum(jnp.sqrt(n), 1e-12) for x, n in zip(kn, n2)]
    bp = [x * y for x, y in zip(kkp, alp)]
    stk = lambda x: _stack(x, m0, m1)
    kg_st = [stk(x * y) for x, y in zip(kkp, gex)]
    rg = [x * y for x, y in zip(rp, gin)]
    bi_st = [stk(x * y) for x, y in zip(bp, ginv)]
    ki_st = [stk(x * y) for x, y in zip(ktp, ginv)]
    kd_st = [stk(x * y) for x, y in zip(ktp, gend)]
    bd_st = [stk(x * y) for x, y in zip(bp, gend)]
    v_st = [stk(x) for x in vp]
    m = [_dot_nt(jnp.concatenate([kg, stk(rg_)], axis=0), jnp.concatenate([bi, ki], axis=0))
         for kg, rg_, bi, ki in zip(kg_st, rg, bi_st, ki_st)]
    a_m = [jnp.where(strict_bd, x[:2 * c, :2 * c], 0.0) for x in m]
    b_m = [jnp.where(strict_bd, x[:2 * c, 2 * c:], 0.0) for x in m]
    q_m = [jnp.where(incl_bd, x[2 * c:, :2 * c], 0.0) for x in m]
    p_m = [jnp.where(incl_bd, x[2 * c:, 2 * c:], 0.0) for x in m]
    t_inv = _unit_lower_inverse_many(a_m)
    bv = [_dot_hi(x, y) for x, y in zip(b_m, v_st)]
    wu = [_dot_hi(t, jnp.concatenate([kg, x], axis=1)) for t, kg, x in zip(t_inv, kg_st, bv)]
    pv = [_dot_hi(x, y) for x, y in zip(p_m, v_st)]
    vkd = [_dot_tn(x, y) for x, y in zip(v_st, kd_st)]
    bonus = [_head_sum(r_ * kt * tile(rk_ref, p), ones_bd) * v_
             for r_, kt, v_, (u, p) in zip(rp, ktp, vp, groups)]

    for j in range(n_sub):
        idx = [i for i, (u, p) in enumerate(groups) if units[u][0] == j]
        bp_ = [(units[groups[i][0]][1], groups[i][1]) for i in idx]
        st = [state_ref[g] for g in bp_]
        u_st = [_dot_nt(wu[i][:, :LANES], s_) + wu[i][:, LANES:] for i, s_ in zip(idx, st)]
        rs = [_dot_nt(rg[i], s_) for i, s_ in zip(idx, st)]
        qu = [_dot_hi(q_m[i], x) for i, x in zip(idx, u_st)]
        ubd = [_dot_tn(x, bd_st[i]) for i, x in zip(idx, u_st)]
        y = [x + _unstack(pv[i] - q_) for i, x, q_ in zip(idx, rs, qu)]
        for n, (i, g) in enumerate(zip(idx, bp_)):
            state_ref[g] = st[n] * gtot[i] + vkd[i] - ubd[n]
        mean = [_head_sum(x, ones_bd) * inv_n for x in y]
        d = [x - mu_ for x, mu_ in zip(y, mean)]
        var = [_head_sum(x * x, ones_bd) * inv_n for x in d]
        for n, (i, (b, p)) in enumerate(zip(idx, bp_)):
            yn = d[n] * lax.rsqrt(var[n] + RWKV_GN_EPS) * tile(lnw_ref, p) + tile(lnb_ref, p)
            o_ref[b, rows(j), L + p * LANES:L + (p + 1) * LANES] = (
                (yn + bonus[i]) * tile(g_out[groups[i][0]], p))
    for b in range(n_batch):
        zprev_ref[b] = z_ref[b, n_sub * c - TAIL:, :]


def _pair_blockdiag(w):
    g = w.reshape(N_PAIRS, PAIR, LRU_BLOCK, LRU_BLOCK)
    z = jnp.zeros((N_PAIRS, LRU_BLOCK, LRU_BLOCK), w.dtype)
    top = jnp.concatenate([g[:, 0], z], axis=2)
    bot = jnp.concatenate([z, g[:, 1]], axis=2)
    return jnp.concatenate([top, bot], axis=1)


def _mixer_ab(z, conv_w, conv_b, ga_w, ga_b, gx_w, gx_b, a_param, mu, w0, w2, a0, a2, g2,
              k_k, k_a, r_k, ln_w, ln_b):
    b, s, n = z.shape
    rows = CHUNKS_PER_STEP * CHUNK
    row = lambda t: t.reshape(1, -1)
    w2p = jnp.concatenate([w2, jnp.zeros((ICLR_RANK, RWKV_WIDTH), F32)], axis=0).astype(BF16)
    a2p = jnp.concatenate([jnp.zeros((DECAY_RANK, RWKV_WIDTH), F32), a2], axis=0).astype(BF16)
    args = (z, conv_w, row(conv_b), _pair_blockdiag(ga_w).astype(BF16), row(ga_b),
            _pair_blockdiag(gx_w).astype(BF16), row(gx_b), row(a_param), row(mu), row(w0), w2p,
            row(a0), a2p, g2.astype(BF16), row(k_k), row(k_a), row(r_k), row(ln_w), row(ln_b))

    def full(a):
        nd = a.ndim
        return pl.BlockSpec(a.shape, lambda i: (0,) * nd)

    return pl.pallas_call(
        _mixer_ab_kernel,
        grid=(s // rows,),
        in_specs=[pl.BlockSpec((b, rows, n), lambda i: (0, i, 0))] + [full(a) for a in args[1:]],
        out_specs=pl.BlockSpec((b, rows, D_MODEL), lambda i: (0, i, 0)),
        out_shape=jax.ShapeDtypeStruct((b, s, D_MODEL), F32),
        scratch_shapes=[
            pltpu.VMEM((b, TAIL, n), F32),
            pltpu.VMEM((b, 8, LRU_WIDTH), F32),
            pltpu.VMEM((b, N_PAIRS, LANES, LANES), F32),
        ],
        compiler_params=pltpu.CompilerParams(
            dimension_semantics=("arbitrary",), vmem_limit_bytes=VMEM_LIMIT),
        name="mixer_ab",
    )(*args)


def _mixer_c_kernel(z_ref, cw_ref, alog_ref, dt_ref, nw_ref, o_ref, zprev_ref, state_ref):
    n_batch = z_ref.shape[0]
    c = CHUNK
    W = DN_WIDTH
    dh = DN_HEAD_DIM

    @pl.when(pl.program_id(0) == 0)
    def _():
        zprev_ref[...] = jnp.zeros_like(zprev_ref)
        state_ref[...] = jnp.zeros_like(state_ref)

    ri = _iota((c, c), 0)
    ci = _iota((c, c), 1)
    causal = ri >= ci
    strict = ri > ci
    l_incl = jnp.where(causal, 1.0, 0.0)

    n_sub = z_ref.shape[1] // c
    units = [(j, b) for j in range(n_sub) for b in range(n_batch)]
    groups = [(u, h) for u in range(len(units)) for h in range(DN_HEADS)]
    rows = lambda j: slice(j * c, (j + 1) * c)
    zq = [z_ref[b, rows(j), :3 * W] for j, b in units]
    tail = [zprev_ref[b] if j == 0 else z_ref[b, j * c - TAIL:j * c, :3 * W] for j, b in units]
    qkv = [_silu(_causal_conv(x, t, cw_ref)) for x, t in zip(zq, tail)]
    zs = [z_ref[b, rows(j), 4 * W:] for j, b in units]
    beta_all = [_sigmoid(x) for x in zs]
    g_all = [-jnp.exp(alog_ref[...]) * _softplus(x + dt_ref[...]) for x in zs]
    gc_all = [_cumsum_rows(l_incl, x) for x in g_all]
    gc_rows = [x.T for x in gc_all]

    beta = [beta_all[u][:, h:h + 1] for u, h in groups]
    gcol = [gc_all[u][:, DN_HEADS + h:DN_HEADS + h + 1] for u, h in groups]
    grow = [gc_rows[u][DN_HEADS + h:DN_HEADS + h + 1, :] for u, h in groups]
    glast = [x[c - 1:c, :] for x in gcol]
    q = [qkv[u][:, h * dh:(h + 1) * dh] for u, h in groups]
    k = [qkv[u][:, W + h * dh:W + (h + 1) * dh] for u, h in groups]
    v = [qkv[u][:, 2 * W + h * dh:2 * W + (h + 1) * dh] for u, h in groups]
    ones = jnp.ones((dh, dh), BF16)
    lane_sum = lambda x: jnp.dot(x.astype(BF16), ones, preferred_element_type=F32)
    q = [x * (lax.rsqrt(lane_sum(x * x) + 1e-6) * (dh ** -0.5)) for x in q]
    k = [x * lax.rsqrt(lane_sum(x * x) + 1e-6) for x in k]
    decay = [jnp.exp(jnp.where(causal, gc - gr, -jnp.inf)) for gc, gr in zip(gcol, grow)]
    kk = [_dot_nt(x, x) for x in k]
    a_m = [jnp.where(strict, bt * m * d, 0.0) for bt, m, d in zip(beta, kk, decay)]
    t_inv = _unit_lower_inverse_many(a_m)
    egc = [jnp.exp(x) for x in gcol]
    rhs = [jnp.concatenate([vv * bt, kx * (bt * e)], axis=1)
           for vv, kx, bt, e in zip(v, k, beta, egc)]
    sol = [_dot_hi(t, r) for t, r in zip(t_inv, rhs)]
    attn = [_dot_nt(qx, kx) * d for qx, kx, d in zip(q, k, decay)]
    qe = [qx * e for qx, e in zip(q, egc)]
    kdec = [kx * jnp.exp(gl - gc) for kx, gl, gc in zip(k, glast, gcol)]
    eg = [jnp.exp(x) for x in glast]

    for j in range(n_sub):
        idx = [i for i, (u, h) in enumerate(groups) if units[u][0] == j]
        bh = [(units[groups[i][0]][1], groups[i][1]) for i in idx]
        st = [state_ref[g] for g in bh]
        ws = [_dot_hi(sol[i][:, dh:], s_) for i, s_ in zip(idx, st)]
        v_new = [sol[i][:, :dh] - y for i, y in zip(idx, ws)]
        qs = [_dot_hi(qe[i], s_) for i, s_ in zip(idx, st)]
        av = [_dot_hi(attn[i], vn) for i, vn in zip(idx, v_new)]
        kv = [_dot_tn(kdec[i], vn) for i, vn in zip(idx, v_new)]
        o = [x + y for x, y in zip(qs, av)]
        ms = [lane_sum(x * x) for x in o]
        for n, (i, (b, h)) in enumerate(zip(idx, bh)):
            state_ref[b, h] = st[n] * eg[i] + kv[n]
            gate = z_ref[b, rows(j), 3 * W + h * dh:3 * W + (h + 1) * dh]
            o_ref[b, rows(j), h * dh:(h + 1) * dh] = o[n] * lax.rsqrt(
                ms[n] * (1.0 / dh) + NORM_EPS) * (nw_ref[...] * _silu(gate))
    for b in range(n_batch):
        zprev_ref[b] = z_ref[b, n_sub * c - TAIL:, :3 * W]


def _mixer_c(z, conv_w, a_log, dt_bias, norm_w):
    b, s, n = z.shape
    rows = CHUNKS_PER_STEP * CHUNK
    pad = jnp.zeros((C_SMALL - 2 * DN_HEADS,), F32)
    lead = jnp.zeros((DN_HEADS,), F32)
    alog_row = jnp.concatenate([lead, a_log, pad]).reshape(1, C_SMALL)
    dt_row = jnp.concatenate([lead, dt_bias, pad]).reshape(1, C_SMALL)
    args = (z, conv_w, alog_row, dt_row, norm_w.reshape(1, DN_HEAD_DIM))

    def full(a):
        nd = a.ndim
        return pl.BlockSpec(a.shape, lambda i: (0,) * nd)

    return pl.pallas_call(
        _mixer_c_kernel,
        grid=(s // rows,),
        in_specs=[pl.BlockSpec((b, rows, n), lambda i: (0, i, 0))] + [full(a) for a in args[1:]],
        out_specs=pl.BlockSpec((b, rows, DN_WIDTH), lambda i: (0, i, 0)),
        out_shape=jax.ShapeDtypeStruct((b, s, DN_WIDTH), F32),
        scratch_shapes=[
            pltpu.VMEM((b, TAIL, 3 * DN_WIDTH), F32),
            pltpu.VMEM((b, DN_HEADS, DN_HEAD_DIM, DN_HEAD_DIM), F32),
        ],
        compiler_params=pltpu.CompilerParams(
            dimension_semantics=("arbitrary",), vmem_limit_bytes=VMEM_LIMIT),
        name="mixer_c",
    )(*args)


def kernel(x, ffn_norm, ffn_w_gate, ffn_w_up, ffn_w_down, mix_norm, final_norm, ab_w_in, lru_conv_w, lru_conv_b, lru_gate_a_w, lru_gate_a_b, lru_gate_x_w, lru_gate_x_b, lru_a_param, rwkv_mu, rwkv_w0, rwkv_w2, rwkv_a0, rwkv_a2, rwkv_g2, rwkv_k_k, rwkv_k_a, rwkv_r_k, rwkv_ln_w, rwkv_ln_b, ab_w_out, c_w_in, dn_conv_w, dn_A_log, dn_dt_bias, dn_norm, c_w_out):
    b, s, d = x.shape
    depth = ffn_norm.shape[0]
    ffn_w = (ffn_w_gate, ffn_w_up, ffn_w_down)
    xf = x.reshape(b * s, d)
    for l in range(depth):
        i = l // 2
        xf = _ffn(xf, ffn_norm[l, 0], *ffn_w, (l, 0), final_norm, final_norm=False)
        if l % 2 == 0:
            z = _norm_proj(xf, mix_norm[l], ab_w_in, i).reshape(b, s, AB_IN)
            y = _mixer_ab(z, lru_conv_w[i], lru_conv_b[i], lru_gate_a_w[i], lru_gate_a_b[i],
                          lru_gate_x_w[i], lru_gate_x_b[i], lru_a_param[i], rwkv_mu[i],
                          rwkv_w0[i], rwkv_w2[i], rwkv_a0[i], rwkv_a2[i], rwkv_g2[i],
                          rwkv_k_k[i], rwkv_k_a[i], rwkv_r_k[i], rwkv_ln_w[i], rwkv_ln_b[i])
            proj = (y.reshape(b * s, D_MODEL), ab_w_out, i)
        else:
            w_in = jnp.pad(c_w_in, ((0, 0), (0, 0), (0, C_IN_PAD - c_w_in.shape[2])))
            z = _norm_proj(xf, mix_norm[l], w_in, i).reshape(b, s, C_IN_PAD)
            y = _mixer_c(z, dn_conv_w[i], dn_A_log[i], dn_dt_bias[i], dn_norm[i])
            proj = (y.reshape(b * s, DN_WIDTH), c_w_out, i)
        xf = _ffn(xf, ffn_norm[l, 1], *ffn_w, (l, 1), final_norm,
                  final_norm=(l == depth - 1), proj=proj)
    return xf.reshape(b, s, d)
```

```python
import functools

import jax
import jax.numpy as jnp
from jax import lax
from jax.experimental import pallas as pl
from jax.experimental.pallas import tpu as pltpu

F32 = jnp.float32
BF16 = jnp.bfloat16

D_MODEL = 1024
D_FF = 2816
NORM_EPS = 1e-6
CONV_WIDTH = 4

LRU_WIDTH = 512
LRU_BLOCK = 64
LRU_C = 8.0

RWKV_WIDTH = 512
RWKV_HEAD_DIM = 64
DECAY_RANK = 64
ICLR_RANK = 64
GATE_RANK = 128
RWKV_GN_EPS = 64e-5
SHIFT_WIDTH = 3 * RWKV_WIDTH + DECAY_RANK + ICLR_RANK + GATE_RANK
AB_IN = 2 * LRU_WIDTH + SHIFT_WIDTH

DN_HEADS = 8
DN_HEAD_DIM = 128
DN_WIDTH = DN_HEADS * DN_HEAD_DIM
C_SMALL = 128
C_IN_PAD = 4 * DN_WIDTH + C_SMALL

LANES = 128
CHUNK = 64
CHUNKS_PER_STEP = 2
TAIL = 8
PAIR = LANES // RWKV_HEAD_DIM
N_PAIRS = RWKV_WIDTH // LANES
INV_LEVELS = 6
VMEM_LIMIT = 56 * 1024 * 1024


def _dot(a, b):
    return jnp.dot(a, b, preferred_element_type=F32)


def _rec_dot(a, b, contract):
    return lax.dot_general(a.astype(BF16), b.astype(BF16), (contract, ((), ())),
                           preferred_element_type=F32)


def _dot_hi(a, b):
    return _rec_dot(a, b, ((1,), (0,)))


def _dot_nt(a, b):
    return _rec_dot(a, b, ((1,), (1,)))


def _dot_tn(a, b):
    return _rec_dot(a, b, ((0,), (0,)))


def _cumsum_rows(l_incl, x):
    lb = l_incl.astype(BF16)
    hi = x.astype(BF16)
    lo = (x - hi.astype(F32)).astype(BF16)
    return _dot(lb, hi) + _dot(lb, lo)


def _head_sum(x, ones_bd):
    return jnp.dot(x.astype(BF16), ones_bd.astype(BF16), preferred_element_type=F32)


def _rmsnorm(x, w, eps=NORM_EPS):
    return x * lax.rsqrt(jnp.mean(x * x, axis=-1, keepdims=True) + eps) * w


def _sigmoid(x):
    return 0.5 + 0.5 * jnp.tanh(0.5 * x)


def _silu(x):
    h = 0.5 * x
    return h + h * jnp.tanh(h)


def _softplus(x):
    return jnp.maximum(x, 0.0) + jnp.log(1.0 + jnp.exp(-jnp.abs(x)))


def _sqrt(x):
    return jnp.where(x > 0.0, x * lax.rsqrt(x), 0.0)


def _gelu_tanh(x):
    return 0.5 * x * (1.0 + jnp.tanh(0.7978845608028654 * (x + 0.044715 * x * x * x)))


def _iota(shape, dim):
    return lax.broadcasted_iota(jnp.int32, shape, dim)


def _with_tail(x, tail):
    return jnp.concatenate([tail, x], axis=0)


def _shift_rows(xt, k):
    return pltpu.roll(xt, k, 0)[TAIL:]


def _causal_conv(x, tail, w_ref):
    xt = _with_tail(x, tail)
    y = w_ref[CONV_WIDTH - 1:CONV_WIDTH, :] * x
    for k in range(1, CONV_WIDTH):
        y = y + w_ref[CONV_WIDTH - 1 - k:CONV_WIDTH - k, :] * _shift_rows(xt, k)
    return y


def _unit_lower_inverse_many(mats):
    eye = jnp.where(_iota(mats[0].shape, 0) == _iota(mats[0].shape, 1), 1.0, 0.0)
    inv = [eye - a for a in mats]
    p = mats
    for _ in range(INV_LEVELS - 1):
        p = [_dot_hi(x, x) for x in p]
        inv = [i + _dot_hi(i, x) for i, x in zip(inv, p)]
    return inv


def _linear_scan(a, b):
    sub = _iota((TAIL, 1), 0)
    tiles = [(a[t:t + TAIL], b[t:t + TAIL]) for t in range(0, a.shape[0], TAIL)]
    k = 1
    while k < TAIL:
        keep = sub >= k
        tiles = [(at * jnp.where(keep, pltpu.roll(at, k, 0), 1.0),
                  at * jnp.where(keep, pltpu.roll(bt, k, 0), 0.0) + bt) for at, bt in tiles]
        k *= 2
    out = [tiles[0]]
    for at, bt in tiles[1:]:
        ca, cb = out[-1]
        out.append((at * ca[TAIL - 1:], bt + at * cb[TAIL - 1:]))
    return jnp.concatenate([x for x, _ in out], axis=0), jnp.concatenate([y for _, y in out], axis=0)


def _resident(block_shape, index):
    return pl.BlockSpec(block_shape, lambda i: index, pipeline_mode=pl.Buffered(1))


def _ffn_kernel(*refs, f_chunk, final_norm, fuse_proj):
    if fuse_proj:
        x_ref, y_ref, wo_ref, nw_ref, wg_ref, wu_ref, wd_ref, fw_ref, o_ref = refs
        x = x_ref[...] + _dot(y_ref[...].astype(BF16), wo_ref[...].astype(BF16))
    else:
        x_ref, nw_ref, wg_ref, wu_ref, wd_ref, fw_ref, o_ref = refs
        x = x_ref[...]
    h = _rmsnorm(x, nw_ref[...]).astype(BF16)
    acc = jnp.zeros_like(x)
    for j in range(D_FF // f_chunk):
        lo = j * f_chunk
        g = _dot(h, wg_ref[:, lo:lo + f_chunk].astype(BF16))
        u = _dot(h, wu_ref[:, lo:lo + f_chunk].astype(BF16))
        acc = acc + _dot((_silu(g) * u).astype(BF16), wd_ref[lo:lo + f_chunk, :].astype(BF16))
    y = x + 0.5 * acc
    if final_norm:
        y = _rmsnorm(y, fw_ref[...])
    o_ref[...] = y


def _ffn(x, norm_w, w_gate, w_up, w_down, lj, final_w, *, final_norm, proj=None,
         tm=512, f_chunk=256):
    t, d = x.shape
    rows = lambda i: (i, 0)
    row_spec = pl.BlockSpec((tm, d), rows)
    vec_spec = pl.BlockSpec((1, d), lambda i: (0, 0))
    in_specs, args = [row_spec], [x]
    if proj is not None:
        y, w_out, i_out = proj
        k = y.shape[1]
        in_specs += [pl.BlockSpec((tm, k), rows), _resident((None, k, d), (i_out, 0, 0))]
        args += [y, w_out]
    in_specs += [vec_spec,
                 _resident((None, None, d, D_FF), lj + (0, 0)),
                 _resident((None, None, d, D_FF), lj + (0, 0)),
                 _resident((None, None, D_FF, d), lj + (0, 0)),
                 vec_spec]
    args += [norm_w.reshape(1, d), w_gate, w_up, w_down, final_w.reshape(1, d)]
    return pl.pallas_call(
        functools.partial(_ffn_kernel, f_chunk=f_chunk, final_norm=final_norm,
                          fuse_proj=proj is not None),
        grid=(t // tm,),
        in_specs=in_specs,
        out_specs=row_spec,
        out_shape=jax.ShapeDtypeStruct((t, d), F32),
        compiler_params=pltpu.CompilerParams(
            dimension_semantics=("arbitrary",), vmem_limit_bytes=VMEM_LIMIT),
        name="ffn_proj" if proj is not None else "ffn",
    )(*args)


def _norm_proj_kernel(x_ref, nw_ref, w_ref, o_ref):
    h = _rmsnorm(x_ref[...], nw_ref[...]).astype(BF16)
    o_ref[...] = _dot(h, w_ref[...].astype(BF16))


def _norm_proj(x, norm_w, w, i_w, *, tm=512):
    t, d = x.shape
    n = w.shape[2]
    return pl.pallas_call(
        _norm_proj_kernel,
        grid=(t // tm,),
        in_specs=[
            pl.BlockSpec((tm, d), lambda i: (i, 0)),
            pl.BlockSpec((1, d), lambda i: (0, 0)),
            _resident((None, d, n), (i_w, 0, 0)),
        ],
        out_specs=pl.BlockSpec((tm, n), lambda i: (i, 0)),
        out_shape=jax.ShapeDtypeStruct((t, n), F32),
        compiler_params=pltpu.CompilerParams(
            dimension_semantics=("arbitrary",), vmem_limit_bytes=VMEM_LIMIT),
        name="norm_proj",
    )(x, norm_w.reshape(1, d), w)


def _norm_proj_t_kernel(x_ref, nw_ref, wm_ref, ws_ref, o_ref, *, n_main):
    h = _rmsnorm(x_ref[...], nw_ref[...]).astype(BF16)
    nt = (((1,), (1,)), ((), ()))
    o_ref[:, :n_main] = lax.dot_general(h, wm_ref[...].astype(BF16), nt,
                                        preferred_element_type=F32)
    ws = ws_ref[...]
    ws = jnp.concatenate([ws, jnp.zeros((C_SMALL - ws.shape[0], ws.shape[1]), F32)], axis=0)
    o_ref[:, n_main:] = lax.dot_general(h, ws.astype(BF16), nt, preferred_element_type=F32)


def _norm_proj_t(x, norm_w, wt, i_w, *, tm=512):
    t, d = x.shape
    n = wt.shape[1]
    n_main = (n // LANES) * LANES
    n_small = n - n_main
    return pl.pallas_call(
        functools.partial(_norm_proj_t_kernel, n_main=n_main),
        grid=(t // tm,),
        in_specs=[
            pl.BlockSpec((tm, d), lambda i: (i, 0)),
            pl.BlockSpec((1, d), lambda i: (0, 0)),
            _resident((None, n_main, d), (i_w, 0, 0)),
            _resident((None, n_small, d), (i_w, n_main // n_small, 0)),
        ],
        out_specs=pl.BlockSpec((tm, n_main + C_SMALL), lambda i: (i, 0)),
        out_shape=jax.ShapeDtypeStruct((t, n_main + C_SMALL), F32),
        compiler_params=pltpu.CompilerParams(
            dimension_semantics=("arbitrary",), vmem_limit_bytes=VMEM_LIMIT),
        name="norm_proj_t",
    )(x, norm_w.reshape(1, d), wt, wt)


def _stack(x, m0, m1):
    return jnp.concatenate([x * m0, x * m1], axis=0)


def _unstack(y):
    return y[:CHUNK] + y[CHUNK:]


def _mixer_ab_kernel(z_ref, cw_ref, cb_ref, wa_ref, ba_ref, wx_ref, bx_ref, ap_ref, mu_ref,
                     w0_ref, w2_ref, a0_ref, a2_ref, g2_ref, kk_ref, ka_ref, rk_ref,
                     lnw_ref, lnb_ref, o_ref, zprev_ref, hlru_ref, state_ref):
    n_batch = z_ref.shape[0]
    c = CHUNK
    L = LRU_WIDTH

    @pl.when(pl.program_id(0) == 0)
    def _():
        zprev_ref[...] = jnp.zeros_like(zprev_ref)
        hlru_ref[...] = jnp.zeros_like(hlru_ref)
        state_ref[...] = jnp.zeros_like(state_ref)

    lane = _iota((1, LANES), 1)
    m0 = jnp.where(lane < RWKV_HEAD_DIM, 1.0, 0.0)
    m1 = 1.0 - m0
    ri = _iota((c, c), 0)
    ci = _iota((c, c), 1)
    l_incl = jnp.where(ri >= ci, 1.0, 0.0)
    si = _iota((LANES, LANES), 0)
    sj = _iota((LANES, LANES), 1)
    same_head = (si // RWKV_HEAD_DIM) == (sj // RWKV_HEAD_DIM)
    ones_bd = jnp.where(same_head, 1.0, 0.0)
    strict_bd = same_head & (si > sj)
    incl_bd = same_head & (si >= sj)
    inv_n = 1.0 / RWKV_HEAD_DIM

    n_sub = z_ref.shape[1] // c
    units = [(j, b) for j in range(n_sub) for b in range(n_batch)]
    nu = range(len(units))
    groups = [(u, p) for u in nu for p in range(N_PAIRS)]
    tile = lambda x, p: x[:, p * LANES:(p + 1) * LANES]
    rows = lambda j: slice(j * c, (j + 1) * c)
    z = [z_ref[b, rows(j), :] for j, b in units]
    zt = [zprev_ref[b] if j == 0 else z_ref[b, j * c - TAIL:j * c, :] for j, b in units]

    xc = [_causal_conv(z[u][:, :L], zt[u][:, :L], cw_ref) + cb_ref[...] for u in nu]
    xcb = [x.astype(BF16) for x in xc]
    sp_a = _softplus(-ap_ref[...])
    r_gate = [_sigmoid(_dot(tile(xcb[u], p), wa_ref[p]) + tile(ba_ref, p)) for u, p in groups]
    i_gate = [_sigmoid(_dot(tile(xcb[u], p), wx_ref[p]) + tile(bx_ref, p)) for u, p in groups]
    log_a = [-LRU_C * rg_ * tile(sp_a, p) for rg_, (u, p) in zip(r_gate, groups)]
    a = [jnp.exp(x) for x in log_a]
    u_in = [_sqrt(jnp.tanh(-la) * (a_ * a_ + 1.0)) * (ig * tile(xc[u], p))
            for la, a_, ig, (u, p) in zip(log_a, a, i_gate, groups)]
    scans = [_linear_scan(a_, u_) for a_, u_ in zip(a, u_in)]
    gelu = [_gelu_tanh(z[u][:, L + p * LANES:L + (p + 1) * LANES]) for u, p in groups]
    for (a_cum, h), gl_, (u, p) in zip(scans, gelu, groups):
        j, b = units[u]
        sl = slice(p * LANES, (p + 1) * LANES)
        h = h + a_cum * hlru_ref[b, 0:1, sl]
        hlru_ref[b, 0:1, sl] = h[c - 1:c, :]
        o_ref[b, rows(j), sl] = h * gl_

    R = RWKV_WIDTH
    s = [x[:, 2 * L:] for x in z]
    pmix = [s[u] + mu_ref[...] * (_shift_rows(_with_tail(s[u], zt[u][:, 2 * L:]), 1) - s[u])
            for u in nu]
    r = [x[:, :R] for x in pmix]
    k = [x[:, R:2 * R] for x in pmix]
    v = [x[:, 2 * R:3 * R] for x in pmix]
    wa_in = [x[:, 3 * R:3 * R + LANES] for x in pmix]
    gl_in = [x[:, 3 * R + LANES:] for x in pmix]
    w_log = [-_softplus(-(w0_ref[...] + _dot(jnp.tanh(x).astype(BF16), w2_ref[...]))) - 0.5
             for x in wa_in]
    lw = [-jnp.exp(x) for x in w_log]
    a_lr = [_sigmoid(a0_ref[...] + _dot(x.astype(BF16), a2_ref[...])) for x in wa_in]
    g_out = [_dot(_sigmoid(x).astype(BF16), g2_ref[...]) for x in gl_in]
    kkr = [x * kk_ref[...] for x in k]
    k_mod = [kx * (1.0 + (al - 1.0) * ka_ref[...]) for kx, al in zip(k, a_lr)]
    cs = [_cumsum_rows(l_incl, x) for x in lw]
    cl = [x[c - 1:c, :] for x in cs]
    g_in = [jnp.exp(x) for x in cs]
    g_ex = [jnp.exp(x - y) for x, y in zip(cs, lw)]
    g_inv = [jnp.exp(-x) for x in cs]
    g_end = [jnp.exp(x - y) for x, y in zip(cl, cs)]
    g_tot = [jnp.exp(x) for x in cl]

    G = lambda xs: [tile(xs[u], p) for u, p in groups]
    rp, vp, ktp, kn, alp = G(r), G(v), G(k_mod), G(kkr), G(a_lr)
    gin, gex, ginv, gend, gtot = G(g_in), G(g_ex), G(g_inv), G(g_end), G(g_tot)
    n2 = [_head_sum(x * x, ones_bd) for x in kn]
    kkp = [x / jnp.maximum(jnp.sqrt(n), 1e-12) for x, n in zip(kn, n2)]
    bp = [x * y for x, y in zip(kkp, alp)]
    stk = lambda x: _stack(x, m0, m1)
    kg_st = [stk(x * y) for x, y in zip(kkp, gex)]
    rg = [x * y for x, y in zip(rp, gin)]
    bi_st = [stk(x * y) for x, y in zip(bp, ginv)]
    ki_st = [stk(x * y) for x, y in zip(ktp, ginv)]
    kd_st = [stk(x * y) for x, y in zip(ktp, gend)]
    bd_st = [stk(x * y) for x, y in zip(bp, gend)]
    v_st = [stk(x) for x in vp]
    m = [_dot_nt(jnp.concatenate([kg, stk(rg_)], axis=0), jnp.concatenate([bi, ki], axis=0))
         for kg, rg_, bi, ki in zip(kg_st, rg, bi_st, ki_st)]
    a_m = [jnp.where(strict_bd, x[:2 * c, :2 * c], 0.0) for x in m]
    b_m = [jnp.where(strict_bd, x[:2 * c, 2 * c:], 0.0) for x in m]
    q_m = [jnp.where(incl_bd, x[2 * c:, :2 * c], 0.0) for x in m]
    p_m = [jnp.where(incl_bd, x[2 * c:, 2 * c:], 0.0) for x in m]
    t_inv = _unit_lower_inverse_many(a_m)
    bv = [_dot_hi(x, y) for x, y in zip(b_m, v_st)]
    wu = [_dot_hi(t, jnp.concatenate([kg, x], axis=1)) for t, kg, x in zip(t_inv, kg_st, bv)]
    pv = [_dot_hi(x, y) for x, y in zip(p_m, v_st)]
    vkd = [_dot_tn(x, y) for x, y in zip(v_st, kd_st)]
    bonus = [_head_sum(r_ * kt * tile(rk_ref, p), ones_bd) * v_
             for r_, kt, v_, (u, p) in zip(rp, ktp, vp, groups)]

    for j in range(n_sub):
        idx = [i for i, (u, p) in enumerate(groups) if units[u][0] == j]
        bp_ = [(units[groups[i][0]][1], groups[i][1]) for i in idx]
        st = [state_ref[g] for g in bp_]
        u_st = [_dot_nt(wu[i][:, :LANES], s_) + wu[i][:, LANES:] for i, s_ in zip(idx, st)]
        rs = [_dot_nt(rg[i], s_) for i, s_ in zip(idx, st)]
        qu = [_dot_hi(q_m[i], x) for i, x in zip(idx, u_st)]
        ubd = [_dot_tn(x, bd_st[i]) for i, x in zip(idx, u_st)]
        y = [x + _unstack(pv[i] - q_) for i, x, q_ in zip(idx, rs, qu)]
        for n, (i, g) in enumerate(zip(idx, bp_)):
            state_ref[g] = st[n] * gtot[i] + vkd[i] - ubd[n]
        mean = [_head_sum(x, ones_bd) * inv_n for x in y]
        d = [x - mu_ for x, mu_ in zip(y, mean)]
        var = [_head_sum(x * x, ones_bd) * inv_n for x in d]
        for n, (i, (b, p)) in enumerate(zip(idx, bp_)):
            yn = d[n] * lax.rsqrt(var[n] + RWKV_GN_EPS) * tile(lnw_ref, p) + tile(lnb_ref, p)
            o_ref[b, rows(j), L + p * LANES:L + (p + 1) * LANES] = (
                (yn + bonus[i]) * tile(g_out[groups[i][0]], p))
    for b in range(n_batch):
        zprev_ref[b] = z_ref[b, n_sub * c - TAIL:, :]


def _pair_blockdiag(w):
    g = w.reshape(N_PAIRS, PAIR, LRU_BLOCK, LRU_BLOCK)
    z = jnp.zeros((N_PAIRS, LRU_BLOCK, LRU_BLOCK), w.dtype)
    top = jnp.concatenate([g[:, 0], z], axis=2)
    bot = jnp.concatenate([z, g[:, 1]], axis=2)
    return jnp.concatenate([top, bot], axis=1)


def _mixer_ab(z, conv_w, conv_b, ga_w, ga_b, gx_w, gx_b, a_param, mu, w0, w2, a0, a2, g2,
              k_k, k_a, r_k, ln_w, ln_b):
    b, s, n = z.shape
    rows = CHUNKS_PER_STEP * CHUNK
    row = lambda t: t.reshape(1, -1)
    w2p = jnp.concatenate([w2, jnp.zeros((ICLR_RANK, RWKV_WIDTH), F32)], axis=0).astype(BF16)
    a2p = jnp.concatenate([jnp.zeros((DECAY_RANK, RWKV_WIDTH), F32), a2], axis=0).astype(BF16)
    args = (z, conv_w, row(conv_b), _pair_blockdiag(ga_w).astype(BF16), row(ga_b),
            _pair_blockdiag(gx_w).astype(BF16), row(gx_b), row(a_param), row(mu), row(w0), w2p,
            row(a0), a2p, g2.astype(BF16), row(k_k), row(k_a), row(r_k), row(ln_w), row(ln_b))

    def full(a):
        nd = a.ndim
        return pl.BlockSpec(a.shape, lambda i: (0,) * nd)

    return pl.pallas_call(
        _mixer_ab_kernel,
        grid=(s // rows,),
        in_specs=[pl.BlockSpec((b, rows, n), lambda i: (0, i, 0))] + [full(a) for a in args[1:]],
        out_specs=pl.BlockSpec((b, rows, D_MODEL), lambda i: (0, i, 0)),
        out_shape=jax.ShapeDtypeStruct((b, s, D_MODEL), F32),
        scratch_shapes=[
            pltpu.VMEM((b, TAIL, n), F32),
            pltpu.VMEM((b, 8, LRU_WIDTH), F32),
            pltpu.VMEM((b, N_PAIRS, LANES, LANES), F32),
        ],
        compiler_params=pltpu.CompilerParams(
            dimension_semantics=("arbitrary",), vmem_limit_bytes=VMEM_LIMIT),
        name="mixer_ab",
    )(*args)


def _mixer_c_kernel(z_ref, cw_ref, alog_ref, dt_ref, nw_ref, o_ref, zprev_ref, state_ref):
    n_batch = z_ref.shape[0]
    c = CHUNK
    W = DN_WIDTH
    dh = DN_HEAD_DIM

    @pl.when(pl.program_id(0) == 0)
    def _():
        zprev_ref[...] = jnp.zeros_like(zprev_ref)
        state_ref[...] = jnp.zeros_like(state_ref)

    ri = _iota((c, c), 0)
    ci = _iota((c, c), 1)
    causal = ri >= ci
    strict = ri > ci
    l_incl = jnp.where(causal, 1.0, 0.0)

    n_sub = z_ref.shape[1] // c
    units = [(j, b) for j in range(n_sub) for b in range(n_batch)]
    groups = [(u, h) for u in range(len(units)) for h in range(DN_HEADS)]
    rows = lambda j: slice(j * c, (j + 1) * c)
    zq = [z_ref[b, rows(j), :3 * W] for j, b in units]
    tail = [zprev_ref[b] if j == 0 else z_ref[b, j * c - TAIL:j * c, :3 * W] for j, b in units]
    qkv = [_silu(_causal_conv(x, t, cw_ref)) for x, t in zip(zq, tail)]
    zs = [z_ref[b, rows(j), 4 * W:] for j, b in units]
    beta_all = [_sigmoid(x) for x in zs]
    g_all = [-jnp.exp(alog_ref[...]) * _softplus(x + dt_ref[...]) for x in zs]
    gc_all = [_cumsum_rows(l_incl, x) for x in g_all]
    gc_rows = [x.T for x in gc_all]

    beta = [beta_all[u][:, h:h + 1] for u, h in groups]
    gcol = [gc_all[u][:, DN_HEADS + h:DN_HEADS + h + 1] for u, h in groups]
    grow = [gc_rows[u][DN_HEADS + h:DN_HEADS + h + 1, :] for u, h in groups]
    glast = [x[c - 1:c, :] for x in gcol]
    q = [qkv[u][:, h * dh:(h + 1) * dh] for u, h in groups]
    k = [qkv[u][:, W + h * dh:W + (h + 1) * dh] for u, h in groups]
    v = [qkv[u][:, 2 * W + h * dh:2 * W + (h + 1) * dh] for u, h in groups]
    ones = jnp.ones((dh, dh), BF16)
    lane_sum = lambda x: jnp.dot(x.astype(BF16), ones, preferred_element_type=F32)
    q = [x * (lax.rsqrt(lane_sum(x * x) + 1e-6) * (dh ** -0.5)) for x in q]
    k = [x * lax.rsqrt(lane_sum(x * x) + 1e-6) for x in k]
    decay = [jnp.exp(jnp.where(causal, gc - gr, -jnp.inf)) for gc, gr in zip(gcol, grow)]
    kk = [_dot_nt(x, x) for x in k]
    a_m = [jnp.where(strict, bt * m * d, 0.0) for bt, m, d in zip(beta, kk, decay)]
    t_inv = _unit_lower_inverse_many(a_m)
    egc = [jnp.exp(x) for x in gcol]
    rhs = [jnp.concatenate([vv * bt, kx * (bt * e)], axis=1)
           for vv, kx, bt, e in zip(v, k, beta, egc)]
    sol = [_dot_hi(t, r) for t, r in zip(t_inv, rhs)]
    attn = [_dot_nt(qx, kx) * d for qx, kx, d in zip(q, k, decay)]
    qe = [qx * e for qx, e in zip(q, egc)]
    kdec = [kx * jnp.exp(gl - gc) for kx, gl, gc in zip(k, glast, gcol)]
    eg = [jnp.exp(x) for x in glast]

    for j in range(n_sub):
        idx = [i for i, (u, h) in enumerate(groups) if units[u][0] == j]
        bh = [(units[groups[i][0]][1], groups[i][1]) for i in idx]
        st = [state_ref[g] for g in bh]
        ws = [_dot_hi(sol[i][:, dh:], s_) for i, s_ in zip(idx, st)]
        v_new = [sol[i][:, :dh] - y for i, y in zip(idx, ws)]
        qs = [_dot_hi(qe[i], s_) for i, s_ in zip(idx, st)]
        av = [_dot_hi(attn[i], vn) for i, vn in zip(idx, v_new)]
        kv = [_dot_tn(kdec[i], vn) for i, vn in zip(idx, v_new)]
        o = [x + y for x, y in zip(qs, av)]
        ms = [lane_sum(x * x) for x in o]
        for n, (i, (b, h)) in enumerate(zip(idx, bh)):
            state_ref[b, h] = st[n] * eg[i] + kv[n]
            gate = z_ref[b, rows(j), 3 * W + h * dh:3 * W + (h + 1) * dh]
            o_ref[b, rows(j), h * dh:(h + 1) * dh] = o[n] * lax.rsqrt(
                ms[n] * (1.0 / dh) + NORM_EPS) * (nw_ref[...] * _silu(gate))
    for b in range(n_batch):
        zprev_ref[b] = z_ref[b, n_sub * c - TAIL:, :3 * W]


def _mixer_c(z, conv_w, a_log, dt_bias, norm_w):
    b, s, n = z.shape
    rows = CHUNKS_PER_STEP * CHUNK
    pad = jnp.zeros((C_SMALL - 2 * DN_HEADS,), F32)
    lead = jnp.zeros((DN_HEADS,), F32)
    alog_row = jnp.concatenate([lead, a_log, pad]).reshape(1, C_SMALL)
    dt_row = jnp.concatenate([lead, dt_bias, pad]).reshape(1, C_SMALL)
    args = (z, conv_w, alog_row, dt_row, norm_w.reshape(1, DN_HEAD_DIM))

    def full(a):
        nd = a.ndim
        return pl.BlockSpec(a.shape, lambda i: (0,) * nd)

    return pl.pallas_call(
        _mixer_c_kernel,
        grid=(s // rows,),
        in_specs=[pl.BlockSpec((b, rows, n), lambda i: (0, i, 0))] + [full(a) for a in args[1:]],
        out_specs=pl.BlockSpec((b, rows, DN_WIDTH), lambda i: (0, i, 0)),
        out_shape=jax.ShapeDtypeStruct((b, s, DN_WIDTH), F32),
        scratch_shapes=[
            pltpu.VMEM((b, TAIL, 3 * DN_WIDTH), F32),
            pltpu.VMEM((b, DN_HEADS, DN_HEAD_DIM, DN_HEAD_DIM), F32),
        ],
        compiler_params=pltpu.CompilerParams(
            dimension_semantics=("arbitrary",), vmem_limit_bytes=VMEM_LIMIT),
        name="mixer_c",
    )(*args)


def kernel(x, ffn_norm, ffn_w_gate, ffn_w_up, ffn_w_down, mix_norm, final_norm, ab_w_in, lru_conv_w, lru_conv_b, lru_gate_a_w, lru_gate_a_b, lru_gate_x_w, lru_gate_x_b, lru_a_param, rwkv_mu, rwkv_w0, rwkv_w2, rwkv_a0, rwkv_a2, rwkv_g2, rwkv_k_k, rwkv_k_a, rwkv_r_k, rwkv_ln_w, rwkv_ln_b, ab_w_out, c_w_in, dn_conv_w, dn_A_log, dn_dt_bias, dn_norm, c_w_out):
    b, s, d = x.shape
    depth = ffn_norm.shape[0]
    ffn_w = (ffn_w_gate, ffn_w_up, ffn_w_down)
    xf = x.reshape(b * s, d)
    for l in range(depth):
        i = l // 2
        xf = _ffn(xf, ffn_norm[l, 0], *ffn_w, (l, 0), final_norm, final_norm=False)
        if l % 2 == 0:
            z = _norm_proj(xf, mix_norm[l], ab_w_in, i).reshape(b, s, AB_IN)
            y = _mixer_ab(z, lru_conv_w[i], lru_conv_b[i], lru_gate_a_w[i], lru_gate_a_b[i],
                          lru_gate_x_w[i], lru_gate_x_b[i], lru_a_param[i], rwkv_mu[i],
                          rwkv_w0[i], rwkv_w2[i], rwkv_a0[i], rwkv_a2[i], rwkv_g2[i],
                          rwkv_k_k[i], rwkv_k_a[i], rwkv_r_k[i], rwkv_ln_w[i], rwkv_ln_b[i])
            proj = (y.reshape(b * s, D_MODEL), ab_w_out, i)
        else:
            z = _norm_proj_t(xf, mix_norm[l], jnp.swapaxes(c_w_in, 1, 2), i).reshape(b, s, C_IN_PAD)
            y = _mixer_c(z, dn_conv_w[i], dn_A_log[i], dn_dt_bias[i], dn_norm[i])
            proj = (y.reshape(b * s, DN_WIDTH), c_w_out, i)
        xf = _ffn(xf, ffn_norm[l, 1], *ffn_w, (l, 1), final_norm,
                  final_norm=(l == depth - 1), proj=proj)
    return xf.reshape(b, s, d)
```

```python
import functools

import jax
import jax.numpy as jnp
from jax import lax
from jax.experimental import pallas as pl
from jax.experimental.pallas import tpu as pltpu

F32 = jnp.float32
BF16 = jnp.bfloat16

D_MODEL = 1024
D_FF = 2816
NORM_EPS = 1e-6
CONV_WIDTH = 4

LRU_WIDTH = 512
LRU_BLOCK = 64
LRU_C = 8.0

RWKV_WIDTH = 512
RWKV_HEAD_DIM = 64
DECAY_RANK = 64
ICLR_RANK = 64
GATE_RANK = 128
RWKV_GN_EPS = 64e-5
SHIFT_WIDTH = 3 * RWKV_WIDTH + DECAY_RANK + ICLR_RANK + GATE_RANK
AB_IN = 2 * LRU_WIDTH + SHIFT_WIDTH

DN_HEADS = 8
DN_HEAD_DIM = 128
DN_WIDTH = DN_HEADS * DN_HEAD_DIM
C_SMALL = 128
C_IN_PAD = 4 * DN_WIDTH + C_SMALL

LANES = 128
CHUNK = 64
CHUNKS_PER_STEP = 4
TAIL = 8
PAIR = LANES // RWKV_HEAD_DIM
N_PAIRS = RWKV_WIDTH // LANES
INV_LEVELS = 6
VMEM_LIMIT = 56 * 1024 * 1024


def _dot(a, b):
    return jnp.dot(a, b, preferred_element_type=F32)


def _rec_dot(a, b, contract):
    return lax.dot_general(a.astype(BF16), b.astype(BF16), (contract, ((), ())),
                           preferred_element_type=F32)


def _dot_hi(a, b):
    return _rec_dot(a, b, ((1,), (0,)))


def _dot_nt(a, b):
    return _rec_dot(a, b, ((1,), (1,)))


def _dot_tn(a, b):
    return _rec_dot(a, b, ((0,), (0,)))


def _cumsum_rows(l_incl, x):
    lb = l_incl.astype(BF16)
    hi = x.astype(BF16)
    lo = (x - hi.astype(F32)).astype(BF16)
    return _dot(lb, hi) + _dot(lb, lo)


def _head_sum(x, ones_bd):
    return jnp.dot(x.astype(BF16), ones_bd.astype(BF16), preferred_element_type=F32)


def _rmsnorm(x, w, eps=NORM_EPS):
    return x * lax.rsqrt(jnp.mean(x * x, axis=-1, keepdims=True) + eps) * w


def _sigmoid(x):
    return 0.5 + 0.5 * jnp.tanh(0.5 * x)


def _silu(x):
    h = 0.5 * x
    return h + h * jnp.tanh(h)


def _softplus(x):
    return jnp.maximum(x, 0.0) + jnp.log(1.0 + jnp.exp(-jnp.abs(x)))


def _sqrt(x):
    return jnp.where(x > 0.0, x * lax.rsqrt(x), 0.0)


def _gelu_tanh(x):
    return 0.5 * x * (1.0 + jnp.tanh(0.7978845608028654 * (x + 0.044715 * x * x * x)))


def _iota(shape, dim):
    return lax.broadcasted_iota(jnp.int32, shape, dim)


def _with_tail(x, tail):
    return jnp.concatenate([tail, x], axis=0)


def _shift_rows(xt, k):
    return pltpu.roll(xt, k, 0)[TAIL:]


def _causal_conv(x, tail, w_ref):
    xt = _with_tail(x, tail)
    y = w_ref[CONV_WIDTH - 1:CONV_WIDTH, :] * x
    for k in range(1, CONV_WIDTH):
        y = y + w_ref[CONV_WIDTH - 1 - k:CONV_WIDTH - k, :] * _shift_rows(xt, k)
    return y


def _unit_lower_inverse_many(mats):
    eye = jnp.where(_iota(mats[0].shape, 0) == _iota(mats[0].shape, 1), 1.0, 0.0)
    inv = [eye - a for a in mats]
    p = mats
    for _ in range(INV_LEVELS - 1):
        p = [_dot_hi(x, x) for x in p]
        inv = [i + _dot_hi(i, x) for i, x in zip(inv, p)]
    return inv


def _linear_scan(a, b):
    sub = _iota((TAIL, 1), 0)
    tiles = [(a[t:t + TAIL], b[t:t + TAIL]) for t in range(0, a.shape[0], TAIL)]
    k = 1
    while k < TAIL:
        keep = sub >= k
        tiles = [(at * jnp.where(keep, pltpu.roll(at, k, 0), 1.0),
                  at * jnp.where(keep, pltpu.roll(bt, k, 0), 0.0) + bt) for at, bt in tiles]
        k *= 2
    out = [tiles[0]]
    for at, bt in tiles[1:]:
        ca, cb = out[-1]
        out.append((at * ca[TAIL - 1:], bt + at * cb[TAIL - 1:]))
    return jnp.concatenate([x for x, _ in out], axis=0), jnp.concatenate([y for _, y in out], axis=0)


def _resident(block_shape, index):
    return pl.BlockSpec(block_shape, lambda i: index, pipeline_mode=pl.Buffered(1))


def _ffn_kernel(*refs, f_chunk, final_norm, fuse_proj):
    if fuse_proj:
        x_ref, y_ref, wo_ref, nw_ref, wg_ref, wu_ref, wd_ref, fw_ref, o_ref = refs
        x = x_ref[...] + _dot(y_ref[...].astype(BF16), wo_ref[...].astype(BF16))
    else:
        x_ref, nw_ref, wg_ref, wu_ref, wd_ref, fw_ref, o_ref = refs
        x = x_ref[...]
    h = _rmsnorm(x, nw_ref[...]).astype(BF16)
    acc = jnp.zeros_like(x)
    for j in range(D_FF // f_chunk):
        lo = j * f_chunk
        g = _dot(h, wg_ref[:, lo:lo + f_chunk].astype(BF16))
        u = _dot(h, wu_ref[:, lo:lo + f_chunk].astype(BF16))
        acc = acc + _dot((_silu(g) * u).astype(BF16), wd_ref[lo:lo + f_chunk, :].astype(BF16))
    y = x + 0.5 * acc
    if final_norm:
        y = _rmsnorm(y, fw_ref[...])
    o_ref[...] = y


def _ffn(x, norm_w, w_gate, w_up, w_down, lj, final_w, *, final_norm, proj=None,
         tm=512, f_chunk=256):
    t, d = x.shape
    rows = lambda i: (i, 0)
    row_spec = pl.BlockSpec((tm, d), rows)
    vec_spec = pl.BlockSpec((1, d), lambda i: (0, 0))
    in_specs, args = [row_spec], [x]
    if proj is not None:
        y, w_out, i_out = proj
        k = y.shape[1]
        in_specs += [pl.BlockSpec((tm, k), rows), _resident((None, k, d), (i_out, 0, 0))]
        args += [y, w_out]
    in_specs += [vec_spec,
                 _resident((None, None, d, D_FF), lj + (0, 0)),
                 _resident((None, None, d, D_FF), lj + (0, 0)),
                 _resident((None, None, D_FF, d), lj + (0, 0)),
                 vec_spec]
    args += [norm_w.reshape(1, d), w_gate, w_up, w_down, final_w.reshape(1, d)]
    return pl.pallas_call(
        functools.partial(_ffn_kernel, f_chunk=f_chunk, final_norm=final_norm,
                          fuse_proj=proj is not None),
        grid=(t // tm,),
        in_specs=in_specs,
        out_specs=row_spec,
        out_shape=jax.ShapeDtypeStruct((t, d), F32),
        compiler_params=pltpu.CompilerParams(
            dimension_semantics=("arbitrary",), vmem_limit_bytes=VMEM_LIMIT),
        name="ffn_proj" if proj is not None else "ffn",
    )(*args)


def _norm_proj_kernel(x_ref, nw_ref, w_ref, o_ref):
    h = _rmsnorm(x_ref[...], nw_ref[...]).astype(BF16)
    o_ref[...] = _dot(h, w_ref[...].astype(BF16))


def _norm_proj(x, norm_w, w, i_w, *, tm=1024):
    t, d = x.shape
    n = w.shape[2]
    return pl.pallas_call(
        _norm_proj_kernel,
        grid=(t // tm,),
        in_specs=[
            pl.BlockSpec((tm, d), lambda i: (i, 0)),
            pl.BlockSpec((1, d), lambda i: (0, 0)),
            _resident((None, d, n), (i_w, 0, 0)),
        ],
        out_specs=pl.BlockSpec((tm, n), lambda i: (i, 0)),
        out_shape=jax.ShapeDtypeStruct((t, n), F32),
        compiler_params=pltpu.CompilerParams(
            dimension_semantics=("arbitrary",), vmem_limit_bytes=VMEM_LIMIT),
        name="norm_proj",
    )(x, norm_w.reshape(1, d), w)


def _norm_proj_t_kernel(x_ref, nw_ref, wm_ref, ws_ref, o_ref, *, n_main):
    h = _rmsnorm(x_ref[...], nw_ref[...]).astype(BF16)
    nt = (((1,), (1,)), ((), ()))
    o_ref[:, :n_main] = lax.dot_general(h, wm_ref[...].astype(BF16), nt,
                                        preferred_element_type=F32)
    ws = ws_ref[...]
    ws = jnp.concatenate([ws, jnp.zeros((C_SMALL - ws.shape[0], ws.shape[1]), F32)], axis=0)
    o_ref[:, n_main:] = lax.dot_general(h, ws.astype(BF16), nt, preferred_element_type=F32)


def _norm_proj_t(x, norm_w, wt, i_w, *, tm=512):
    t, d = x.shape
    n = wt.shape[1]
    n_main = (n // LANES) * LANES
    n_small = n - n_main
    return pl.pallas_call(
        functools.partial(_norm_proj_t_kernel, n_main=n_main),
        grid=(t // tm,),
        in_specs=[
            pl.BlockSpec((tm, d), lambda i: (i, 0)),
            pl.BlockSpec((1, d), lambda i: (0, 0)),
            _resident((None, n_main, d), (i_w, 0, 0)),
            _resident((None, n_small, d), (i_w, n_main // n_small, 0)),
        ],
        out_specs=pl.BlockSpec((tm, n_main + C_SMALL), lambda i: (i, 0)),
        out_shape=jax.ShapeDtypeStruct((t, n_main + C_SMALL), F32),
        compiler_params=pltpu.CompilerParams(
            dimension_semantics=("arbitrary",), vmem_limit_bytes=VMEM_LIMIT),
        name="norm_proj_t",
    )(x, norm_w.reshape(1, d), wt, wt)


def _stack(x, m0, m1):
    return jnp.concatenate([x * m0, x * m1], axis=0)


def _unstack(y):
    return y[:CHUNK] + y[CHUNK:]


def _mixer_ab_kernel(z_ref, cw_ref, cb_ref, wa_ref, ba_ref, wx_ref, bx_ref, ap_ref, mu_ref,
                     w0_ref, w2_ref, a0_ref, a2_ref, g2_ref, kk_ref, ka_ref, rk_ref,
                     lnw_ref, lnb_ref, o_ref, zprev_ref, hlru_ref, state_ref):
    n_batch = z_ref.shape[0]
    c = CHUNK
    L = LRU_WIDTH

    @pl.when(pl.program_id(0) == 0)
    def _():
        zprev_ref[...] = jnp.zeros_like(zprev_ref)
        hlru_ref[...] = jnp.zeros_like(hlru_ref)
        state_ref[...] = jnp.zeros_like(state_ref)

    lane = _iota((1, LANES), 1)
    m0 = jnp.where(lane < RWKV_HEAD_DIM, 1.0, 0.0)
    m1 = 1.0 - m0
    ri = _iota((c, c), 0)
    ci = _iota((c, c), 1)
    l_incl = jnp.where(ri >= ci, 1.0, 0.0)
    si = _iota((LANES, LANES), 0)
    sj = _iota((LANES, LANES), 1)
    same_head = (si // RWKV_HEAD_DIM) == (sj // RWKV_HEAD_DIM)
    ones_bd = jnp.where(same_head, 1.0, 0.0)
    strict_bd = same_head & (si > sj)
    incl_bd = same_head & (si >= sj)
    inv_n = 1.0 / RWKV_HEAD_DIM

    n_sub = z_ref.shape[1] // c
    units = [(j, b) for j in range(n_sub) for b in range(n_batch)]
    nu = range(len(units))
    groups = [(u, p) for u in nu for p in range(N_PAIRS)]
    tile = lambda x, p: x[:, p * LANES:(p + 1) * LANES]
    rows = lambda j: slice(j * c, (j + 1) * c)
    z = [z_ref[b, rows(j), :] for j, b in units]
    zt = [zprev_ref[b] if j == 0 else z_ref[b, j * c - TAIL:j * c, :] for j, b in units]

    xc = [_causal_conv(z[u][:, :L], zt[u][:, :L], cw_ref) + cb_ref[...] for u in nu]
    xcb = [x.astype(BF16) for x in xc]
    sp_a = _softplus(-ap_ref[...])
    r_gate = [_sigmoid(_dot(tile(xcb[u], p), wa_ref[p]) + tile(ba_ref, p)) for u, p in groups]
    i_gate = [_sigmoid(_dot(tile(xcb[u], p), wx_ref[p]) + tile(bx_ref, p)) for u, p in groups]
    log_a = [-LRU_C * rg_ * tile(sp_a, p) for rg_, (u, p) in zip(r_gate, groups)]
    a = [jnp.exp(x) for x in log_a]
    u_in = [_sqrt(jnp.tanh(-la) * (a_ * a_ + 1.0)) * (ig * tile(xc[u], p))
            for la, a_, ig, (u, p) in zip(log_a, a, i_gate, groups)]
    scans = [_linear_scan(a_, u_) for a_, u_ in zip(a, u_in)]
    gelu = [_gelu_tanh(z[u][:, L + p * LANES:L + (p + 1) * LANES]) for u, p in groups]
    for (a_cum, h), gl_, (u, p) in zip(scans, gelu, groups):
        j, b = units[u]
        sl = slice(p * LANES, (p + 1) * LANES)
        h = h + a_cum * hlru_ref[b, 0:1, sl]
        hlru_ref[b, 0:1, sl] = h[c - 1:c, :]
        o_ref[b, rows(j), sl] = h * gl_

    R = RWKV_WIDTH
    s = [x[:, 2 * L:] for x in z]
    pmix = [s[u] + mu_ref[...] * (_shift_rows(_with_tail(s[u], zt[u][:, 2 * L:]), 1) - s[u])
            for u in nu]
    r = [x[:, :R] for x in pmix]
    k = [x[:, R:2 * R] for x in pmix]
    v = [x[:, 2 * R:3 * R] for x in pmix]
    wa_in = [x[:, 3 * R:3 * R + LANES] for x in pmix]
    gl_in = [x[:, 3 * R + LANES:] for x in pmix]
    w_log = [-_softplus(-(w0_ref[...] + _dot(jnp.tanh(x).astype(BF16), w2_ref[...]))) - 0.5
             for x in wa_in]
    lw = [-jnp.exp(x) for x in w_log]
    a_lr = [_sigmoid(a0_ref[...] + _dot(x.astype(BF16), a2_ref[...])) for x in wa_in]
    g_out = [_dot(_sigmoid(x).astype(BF16), g2_ref[...]) for x in gl_in]
    kkr = [x * kk_ref[...] for x in k]
    k_mod = [kx * (1.0 + (al - 1.0) * ka_ref[...]) for kx, al in zip(k, a_lr)]
    cs = [_cumsum_rows(l_incl, x) for x in lw]
    cl = [x[c - 1:c, :] for x in cs]
    g_in = [jnp.exp(x) for x in cs]
    g_ex = [jnp.exp(x - y) for x, y in zip(cs, lw)]
    g_inv = [jnp.exp(-x) for x in cs]
    g_end = [jnp.exp(x - y) for x, y in zip(cl, cs)]
    g_tot = [jnp.exp(x) for x in cl]

    G = lambda xs: [tile(xs[u], p) for u, p in groups]
    rp, vp, ktp, kn, alp = G(r), G(v), G(k_mod), G(kkr), G(a_lr)
    gin, gex, ginv, gend, gtot = G(g_in), G(g_ex), G(g_inv), G(g_end), G(g_tot)
    n2 = [_head_sum(x * x, ones_bd) for x in kn]
    kkp = [x / jnp.maximum(jnp.sqrt(n), 1e-12) for x, n in zip(kn, n2)]
    bp = [x * y for x, y in zip(kkp, alp)]
    stk = lambda x: _stack(x, m0, m1)
    kg_st = [stk(x * y) for x, y in zip(kkp, gex)]
    rg = [x * y for x, y in zip(rp, gin)]
    bi_st = [stk(x * y) for x, y in zip(bp, ginv)]
    ki_st = [stk(x * y) for x, y in zip(ktp, ginv)]
    kd_st = [stk(x * y) for x, y in zip(ktp, gend)]
    bd_st = [stk(x * y) for x, y in zip(bp, gend)]
    v_st = [stk(x) for x in vp]
    m = [_dot_nt(jnp.concatenate([kg, stk(rg_)], axis=0), jnp.concatenate([bi, ki], axis=0))
         for kg, rg_, bi, ki in zip(kg_st, rg, bi_st, ki_st)]
    a_m = [jnp.where(strict_bd, x[:2 * c, :2 * c], 0.0) for x in m]
    b_m = [jnp.where(strict_bd, x[:2 * c, 2 * c:], 0.0) for x in m]
    q_m = [jnp.where(incl_bd, x[2 * c:, :2 * c], 0.0) for x in m]
    p_m = [jnp.where(incl_bd, x[2 * c:, 2 * c:], 0.0) for x in m]
    t_inv = _unit_lower_inverse_many(a_m)
    bv = [_dot_hi(x, y) for x, y in zip(b_m, v_st)]
    wu = [_dot_hi(t, jnp.concatenate([kg, x], axis=1)) for t, kg, x in zip(t_inv, kg_st, bv)]
    pv = [_dot_hi(x, y) for x, y in zip(p_m, v_st)]
    vkd = [_dot_tn(x, y) for x, y in zip(v_st, kd_st)]
    bonus = [_head_sum(r_ * kt * tile(rk_ref, p), ones_bd) * v_
             for r_, kt, v_, (u, p) in zip(rp, ktp, vp, groups)]

    for j in range(n_sub):
        idx = [i for i, (u, p) in enumerate(groups) if units[u][0] == j]
        bp_ = [(units[groups[i][0]][1], groups[i][1]) for i in idx]
        st = [state_ref[g] for g in bp_]
        u_st = [_dot_nt(wu[i][:, :LANES], s_) + wu[i][:, LANES:] for i, s_ in zip(idx, st)]
        rs = [_dot_nt(rg[i], s_) for i, s_ in zip(idx, st)]
        qu = [_dot_hi(q_m[i], x) for i, x in zip(idx, u_st)]
        ubd = [_dot_tn(x, bd_st[i]) for i, x in zip(idx, u_st)]
        y = [x + _unstack(pv[i] - q_) for i, x, q_ in zip(idx, rs, qu)]
        for n, (i, g) in enumerate(zip(idx, bp_)):
            state_ref[g] = st[n] * gtot[i] + vkd[i] - ubd[n]
        mean = [_head_sum(x, ones_bd) * inv_n for x in y]
        d = [x - mu_ for x, mu_ in zip(y, mean)]
        var = [_head_sum(x * x, ones_bd) * inv_n for x in d]
        for n, (i, (b, p)) in enumerate(zip(idx, bp_)):
            yn = d[n] * lax.rsqrt(var[n] + RWKV_GN_EPS) * tile(lnw_ref, p) + tile(lnb_ref, p)
            o_ref[b, rows(j), L + p * LANES:L + (p + 1) * LANES] = (
                (yn + bonus[i]) * tile(g_out[groups[i][0]], p))
    for b in range(n_batch):
        zprev_ref[b] = z_ref[b, n_sub * c - TAIL:, :]


def _pair_blockdiag(w):
    g = w.reshape(N_PAIRS, PAIR, LRU_BLOCK, LRU_BLOCK)
    z = jnp.zeros((N_PAIRS, LRU_BLOCK, LRU_BLOCK), w.dtype)
    top = jnp.concatenate([g[:, 0], z], axis=2)
    bot = jnp.concatenate([z, g[:, 1]], axis=2)
    return jnp.concatenate([top, bot], axis=1)


def _mixer_ab(z, conv_w, conv_b, ga_w, ga_b, gx_w, gx_b, a_param, mu, w0, w2, a0, a2, g2,
              k_k, k_a, r_k, ln_w, ln_b):
    b, s, n = z.shape
    rows = CHUNKS_PER_STEP * CHUNK
    row = lambda t: t.reshape(1, -1)
    w2p = jnp.concatenate([w2, jnp.zeros((ICLR_RANK, RWKV_WIDTH), F32)], axis=0).astype(BF16)
    a2p = jnp.concatenate([jnp.zeros((DECAY_RANK, RWKV_WIDTH), F32), a2], axis=0).astype(BF16)
    args = (z, conv_w, row(conv_b), _pair_blockdiag(ga_w).astype(BF16), row(ga_b),
            _pair_blockdiag(gx_w).astype(BF16), row(gx_b), row(a_param), row(mu), row(w0), w2p,
            row(a0), a2p, g2.astype(BF16), row(k_k), row(k_a), row(r_k), row(ln_w), row(ln_b))

    def full(a):
        nd = a.ndim
        return pl.BlockSpec(a.shape, lambda i: (0,) * nd)

    return pl.pallas_call(
        _mixer_ab_kernel,
        grid=(s // rows,),
        in_specs=[pl.BlockSpec((b, rows, n), lambda i: (0, i, 0))] + [full(a) for a in args[1:]],
        out_specs=pl.BlockSpec((b, rows, D_MODEL), lambda i: (0, i, 0)),
        out_shape=jax.ShapeDtypeStruct((b, s, D_MODEL), F32),
        scratch_shapes=[
            pltpu.VMEM((b, TAIL, n), F32),
            pltpu.VMEM((b, 8, LRU_WIDTH), F32),
            pltpu.VMEM((b, N_PAIRS, LANES, LANES), F32),
        ],
        compiler_params=pltpu.CompilerParams(
            dimension_semantics=("arbitrary",), vmem_limit_bytes=VMEM_LIMIT),
        name="mixer_ab",
    )(*args)


def _mixer_c_kernel(z_ref, cw_ref, alog_ref, dt_ref, nw_ref, o_ref, zprev_ref, state_ref):
    n_batch = z_ref.shape[0]
    c = CHUNK
    W = DN_WIDTH
    dh = DN_HEAD_DIM

    @pl.when(pl.program_id(0) == 0)
    def _():
        zprev_ref[...] = jnp.zeros_like(zprev_ref)
        state_ref[...] = jnp.zeros_like(state_ref)

    ri = _iota((c, c), 0)
    ci = _iota((c, c), 1)
    causal = ri >= ci
    strict = ri > ci
    l_incl = jnp.where(causal, 1.0, 0.0)

    n_sub = z_ref.shape[1] // c
    units = [(j, b) for j in range(n_sub) for b in range(n_batch)]
    groups = [(u, h) for u in range(len(units)) for h in range(DN_HEADS)]
    rows = lambda j: slice(j * c, (j + 1) * c)
    zq = [z_ref[b, rows(j), :3 * W] for j, b in units]
    tail = [zprev_ref[b] if j == 0 else z_ref[b, j * c - TAIL:j * c, :3 * W] for j, b in units]
    qkv = [_silu(_causal_conv(x, t, cw_ref)) for x, t in zip(zq, tail)]
    zs = [z_ref[b, rows(j), 4 * W:] for j, b in units]
    beta_all = [_sigmoid(x) for x in zs]
    g_all = [-jnp.exp(alog_ref[...]) * _softplus(x + dt_ref[...]) for x in zs]
    gc_all = [_cumsum_rows(l_incl, x) for x in g_all]
    gc_rows = [x.T for x in gc_all]

    beta = [beta_all[u][:, h:h + 1] for u, h in groups]
    gcol = [gc_all[u][:, DN_HEADS + h:DN_HEADS + h + 1] for u, h in groups]
    grow = [gc_rows[u][DN_HEADS + h:DN_HEADS + h + 1, :] for u, h in groups]
    glast = [x[c - 1:c, :] for x in gcol]
    q = [qkv[u][:, h * dh:(h + 1) * dh] for u, h in groups]
    k = [qkv[u][:, W + h * dh:W + (h + 1) * dh] for u, h in groups]
    v = [qkv[u][:, 2 * W + h * dh:2 * W + (h + 1) * dh] for u, h in groups]
    ones = jnp.ones((dh, dh), BF16)
    lane_sum = lambda x: jnp.dot(x.astype(BF16), ones, preferred_element_type=F32)
    q = [x * (lax.rsqrt(lane_sum(x * x) + 1e-6) * (dh ** -0.5)) for x in q]
    k = [x * lax.rsqrt(lane_sum(x * x) + 1e-6) for x in k]
    decay = [jnp.exp(jnp.where(causal, gc - gr, -jnp.inf)) for gc, gr in zip(gcol, grow)]
    kk = [_dot_nt(x, x) for x in k]
    a_m = [jnp.where(strict, bt * m * d, 0.0) for bt, m, d in zip(beta, kk, decay)]
    t_inv = _unit_lower_inverse_many(a_m)
    egc = [jnp.exp(x) for x in gcol]
    rhs = [jnp.concatenate([vv * bt, kx * (bt * e)], axis=1)
           for vv, kx, bt, e in zip(v, k, beta, egc)]
    sol = [_dot_hi(t, r) for t, r in zip(t_inv, rhs)]
    attn = [_dot_nt(qx, kx) * d for qx, kx, d in zip(q, k, decay)]
    qe = [qx * e for qx, e in zip(q, egc)]
    kdec = [kx * jnp.exp(gl - gc) for kx, gl, gc in zip(k, glast, gcol)]
    eg = [jnp.exp(x) for x in glast]

    for j in range(n_sub):
        idx = [i for i, (u, h) in enumerate(groups) if units[u][0] == j]
        bh = [(units[groups[i][0]][1], groups[i][1]) for i in idx]
        st = [state_ref[g] for g in bh]
        ws = [_dot_hi(sol[i][:, dh:], s_) for i, s_ in zip(idx, st)]
        v_new = [sol[i][:, :dh] - y for i, y in zip(idx, ws)]
        qs = [_dot_hi(qe[i], s_) for i, s_ in zip(idx, st)]
        av = [_dot_hi(attn[i], vn) for i, vn in zip(idx, v_new)]
        kv = [_dot_tn(kdec[i], vn) for i, vn in zip(idx, v_new)]
        o = [x + y for x, y in zip(qs, av)]
        ms = [lane_sum(x * x) for x in o]
        for n, (i, (b, h)) in enumerate(zip(idx, bh)):
            state_ref[b, h] = st[n] * eg[i] + kv[n]
            gate = z_ref[b, rows(j), 3 * W + h * dh:3 * W + (h + 1) * dh]
            o_ref[b, rows(j), h * dh:(h + 1) * dh] = o[n] * lax.rsqrt(
                ms[n] * (1.0 / dh) + NORM_EPS) * (nw_ref[...] * _silu(gate))
    for b in range(n_batch):
        zprev_ref[b] = z_ref[b, n_sub * c - TAIL:, :3 * W]


def _mixer_c(z, conv_w, a_log, dt_bias, norm_w):
    b, s, n = z.shape
    rows = CHUNKS_PER_STEP * CHUNK
    pad = jnp.zeros((C_SMALL - 2 * DN_HEADS,), F32)
    lead = jnp.zeros((DN_HEADS,), F32)
    alog_row = jnp.concatenate([lead, a_log, pad]).reshape(1, C_SMALL)
    dt_row = jnp.concatenate([lead, dt_bias, pad]).reshape(1, C_SMALL)
    args = (z, conv_w, alog_row, dt_row, norm_w.reshape(1, DN_HEAD_DIM))

    def full(a):
        nd = a.ndim
        return pl.BlockSpec(a.shape, lambda i: (0,) * nd)

    return pl.pallas_call(
        _mixer_c_kernel,
        grid=(s // rows,),
        in_specs=[pl.BlockSpec((b, rows, n), lambda i: (0, i, 0))] + [full(a) for a in args[1:]],
        out_specs=pl.BlockSpec((b, rows, DN_WIDTH), lambda i: (0, i, 0)),
        out_shape=jax.ShapeDtypeStruct((b, s, DN_WIDTH), F32),
        scratch_shapes=[
            pltpu.VMEM((b, TAIL, 3 * DN_WIDTH), F32),
            pltpu.VMEM((b, DN_HEADS, DN_HEAD_DIM, DN_HEAD_DIM), F32),
        ],
        compiler_params=pltpu.CompilerParams(
            dimension_semantics=("arbitrary",), vmem_limit_bytes=VMEM_LIMIT),
        name="mixer_c",
    )(*args)


def kernel(x, ffn_norm, ffn_w_gate, ffn_w_up, ffn_w_down, mix_norm, final_norm, ab_w_in, lru_conv_w, lru_conv_b, lru_gate_a_w, lru_gate_a_b, lru_gate_x_w, lru_gate_x_b, lru_a_param, rwkv_mu, rwkv_w0, rwkv_w2, rwkv_a0, rwkv_a2, rwkv_g2, rwkv_k_k, rwkv_k_a, rwkv_r_k, rwkv_ln_w, rwkv_ln_b, ab_w_out, c_w_in, dn_conv_w, dn_A_log, dn_dt_bias, dn_norm, c_w_out):
    b, s, d = x.shape
    depth = ffn_norm.shape[0]
    ffn_w = (ffn_w_gate, ffn_w_up, ffn_w_down)
    xf = x.reshape(b * s, d)
    for l in range(depth):
        i = l // 2
        xf = _ffn(xf, ffn_norm[l, 0], *ffn_w, (l, 0), final_norm, final_norm=False)
        if l % 2 == 0:
            z = _norm_proj(xf, mix_norm[l], ab_w_in, i).reshape(b, s, AB_IN)
            y = _mixer_ab(z, lru_conv_w[i], lru_conv_b[i], lru_gate_a_w[i], lru_gate_a_b[i],
                          lru_gate_x_w[i], lru_gate_x_b[i], lru_a_param[i], rwkv_mu[i],
                          rwkv_w0[i], rwkv_w2[i], rwkv_a0[i], rwkv_a2[i], rwkv_g2[i],
                          rwkv_k_k[i], rwkv_k_a[i], rwkv_r_k[i], rwkv_ln_w[i], rwkv_ln_b[i])
            proj = (y.reshape(b * s, D_MODEL), ab_w_out, i)
        else:
            z = _norm_proj_t(xf, mix_norm[l], jnp.swapaxes(c_w_in, 1, 2), i).reshape(b, s, C_IN_PAD)
            y = _mixer_c(z, dn_conv_w[i], dn_A_log[i], dn_dt_bias[i], dn_norm[i])
            proj = (y.reshape(b * s, DN_WIDTH), c_w_out, i)
        xf = _ffn(xf, ffn_norm[l, 1], *ffn_w, (l, 1), final_norm,
                  final_norm=(l == depth - 1), proj=proj)
    return xf.reshape(b, s, d)
```

```python
import functools

import jax
import jax.numpy as jnp
from jax import lax
from jax.experimental import pallas as pl
from jax.experimental.pallas import tpu as pltpu

F32 = jnp.float32
BF16 = jnp.bfloat16

D_MODEL = 1024
D_FF = 2816
NORM_EPS = 1e-6
CONV_WIDTH = 4

LRU_WIDTH = 512
LRU_BLOCK = 64
LRU_C = 8.0

RWKV_WIDTH = 512
RWKV_HEAD_DIM = 64
DECAY_RANK = 64
ICLR_RANK = 64
GATE_RANK = 128
RWKV_GN_EPS = 64e-5
SHIFT_WIDTH = 3 * RWKV_WIDTH + DECAY_RANK + ICLR_RANK + GATE_RANK
AB_IN = 2 * LRU_WIDTH + SHIFT_WIDTH

DN_HEADS = 8
DN_HEAD_DIM = 128
DN_WIDTH = DN_HEADS * DN_HEAD_DIM
C_SMALL = 128
C_IN_PAD = 4 * DN_WIDTH + C_SMALL

LANES = 128
CHUNK = 64
CHUNKS_PER_STEP = 4
TAIL = 8
PAIR = LANES // RWKV_HEAD_DIM
N_PAIRS = RWKV_WIDTH // LANES
INV_LEVELS = 6
VMEM_LIMIT = 56 * 1024 * 1024


def _dot(a, b):
    return jnp.dot(a, b, preferred_element_type=F32)


def _rec_dot(a, b, contract):
    return lax.dot_general(a.astype(BF16), b.astype(BF16), (contract, ((), ())),
                           preferred_element_type=F32)


def _dot_hi(a, b):
    return _rec_dot(a, b, ((1,), (0,)))


def _dot_nt(a, b):
    return _rec_dot(a, b, ((1,), (1,)))


def _dot_tn(a, b):
    return _rec_dot(a, b, ((0,), (0,)))


def _cumsum_rows(l_incl, x):
    lb = l_incl.astype(BF16)
    hi = x.astype(BF16)
    lo = (x - hi.astype(F32)).astype(BF16)
    return _dot(lb, hi) + _dot(lb, lo)


def _head_sum(x, ones_bd):
    return jnp.dot(x.astype(BF16), ones_bd.astype(BF16), preferred_element_type=F32)


def _rmsnorm(x, w, eps=NORM_EPS):
    return x * lax.rsqrt(jnp.mean(x * x, axis=-1, keepdims=True) + eps) * w


def _sigmoid(x):
    return 0.5 + 0.5 * jnp.tanh(0.5 * x)


def _silu(x):
    h = 0.5 * x
    return h + h * jnp.tanh(h)


def _softplus(x):
    return jnp.maximum(x, 0.0) + jnp.log(1.0 + jnp.exp(-jnp.abs(x)))


def _sqrt(x):
    return jnp.where(x > 0.0, x * lax.rsqrt(x), 0.0)


def _gelu_tanh(x):
    return 0.5 * x * (1.0 + jnp.tanh(0.7978845608028654 * (x + 0.044715 * x * x * x)))


def _iota(shape, dim):
    return lax.broadcasted_iota(jnp.int32, shape, dim)


def _with_tail(x, tail):
    return jnp.concatenate([tail, x], axis=0)


def _shift_rows(xt, k):
    return pltpu.roll(xt, k, 0)[TAIL:]


def _causal_conv(x, tail, w_ref):
    xt = _with_tail(x, tail)
    y = w_ref[CONV_WIDTH - 1:CONV_WIDTH, :] * x
    for k in range(1, CONV_WIDTH):
        y = y + w_ref[CONV_WIDTH - 1 - k:CONV_WIDTH - k, :] * _shift_rows(xt, k)
    return y


def _unit_lower_inverse_many(mats):
    eye = jnp.where(_iota(mats[0].shape, 0) == _iota(mats[0].shape, 1), 1.0, 0.0)
    inv = [eye - a for a in mats]
    p = mats
    for _ in range(INV_LEVELS - 1):
        p = [_dot_hi(x, x) for x in p]
        inv = [i + _dot_hi(i, x) for i, x in zip(inv, p)]
    return inv


def _linear_scan(a, b):
    sub = _iota((TAIL, 1), 0)
    tiles = [(a[t:t + TAIL], b[t:t + TAIL]) for t in range(0, a.shape[0], TAIL)]
    k = 1
    while k < TAIL:
        keep = sub >= k
        tiles = [(at * jnp.where(keep, pltpu.roll(at, k, 0), 1.0),
                  at * jnp.where(keep, pltpu.roll(bt, k, 0), 0.0) + bt) for at, bt in tiles]
        k *= 2
    out = [tiles[0]]
    for at, bt in tiles[1:]:
        ca, cb = out[-1]
        out.append((at * ca[TAIL - 1:], bt + at * cb[TAIL - 1:]))
    return jnp.concatenate([x for x, _ in out], axis=0), jnp.concatenate([y for _, y in out], axis=0)


def _resident(block_shape, index):
    return pl.BlockSpec(block_shape, lambda i: index, pipeline_mode=pl.Buffered(1))


def _ffn_kernel(*refs, f_chunk, final_norm, fuse_proj):
    if fuse_proj:
        x_ref, y_ref, wo_ref, nw_ref, wg_ref, wu_ref, wd_ref, fw_ref, o_ref = refs
        x = x_ref[...] + _dot(y_ref[...].astype(BF16), wo_ref[...].astype(BF16))
    else:
        x_ref, nw_ref, wg_ref, wu_ref, wd_ref, fw_ref, o_ref = refs
        x = x_ref[...]
    h = _rmsnorm(x, nw_ref[...]).astype(BF16)
    acc = jnp.zeros_like(x)
    for j in range(D_FF // f_chunk):
        lo = j * f_chunk
        g = _dot(h, wg_ref[:, lo:lo + f_chunk].astype(BF16))
        u = _dot(h, wu_ref[:, lo:lo + f_chunk].astype(BF16))
        acc = acc + _dot((_silu(g) * u).astype(BF16), wd_ref[lo:lo + f_chunk, :].astype(BF16))
    y = x + 0.5 * acc
    if final_norm:
        y = _rmsnorm(y, fw_ref[...])
    o_ref[...] = y


def _ffn(x, norm_w, w_gate, w_up, w_down, lj, final_w, *, final_norm, proj=None,
         tm=512, f_chunk=256):
    t, d = x.shape
    assert t % tm == 0 and D_FF % f_chunk == 0 and w_gate.shape[2:] == (d, D_FF), (x.shape, tm)
    rows = lambda i: (i, 0)
    row_spec = pl.BlockSpec((tm, d), rows)
    vec_spec = pl.BlockSpec((1, d), lambda i: (0, 0))
    in_specs, args = [row_spec], [x]
    if proj is not None:
        y, w_out, i_out = proj
        k = y.shape[1]
        in_specs += [pl.BlockSpec((tm, k), rows), _resident((None, k, d), (i_out, 0, 0))]
        args += [y, w_out]
    in_specs += [vec_spec,
                 _resident((None, None, d, D_FF), lj + (0, 0)),
                 _resident((None, None, d, D_FF), lj + (0, 0)),
                 _resident((None, None, D_FF, d), lj + (0, 0)),
                 vec_spec]
    args += [norm_w.reshape(1, d), w_gate, w_up, w_down, final_w.reshape(1, d)]
    return pl.pallas_call(
        functools.partial(_ffn_kernel, f_chunk=f_chunk, final_norm=final_norm,
                          fuse_proj=proj is not None),
        grid=(t // tm,),
        in_specs=in_specs,
        out_specs=row_spec,
        out_shape=jax.ShapeDtypeStruct((t, d), F32),
        compiler_params=pltpu.CompilerParams(
            dimension_semantics=("arbitrary",), vmem_limit_bytes=VMEM_LIMIT),
        name="ffn_proj" if proj is not None else "ffn",
    )(*args)


def _norm_proj_kernel(x_ref, nw_ref, w_ref, o_ref):
    h = _rmsnorm(x_ref[...], nw_ref[...]).astype(BF16)
    o_ref[...] = _dot(h, w_ref[...].astype(BF16))


def _norm_proj(x, norm_w, w, i_w, *, tm=1024):
    t, d = x.shape
    n = w.shape[2]
    assert t % tm == 0 and n % LANES == 0 and w.shape[1] == d, (x.shape, w.shape, tm)
    return pl.pallas_call(
        _norm_proj_kernel,
        grid=(t // tm,),
        in_specs=[
            pl.BlockSpec((tm, d), lambda i: (i, 0)),
            pl.BlockSpec((1, d), lambda i: (0, 0)),
            _resident((None, d, n), (i_w, 0, 0)),
        ],
        out_specs=pl.BlockSpec((tm, n), lambda i: (i, 0)),
        out_shape=jax.ShapeDtypeStruct((t, n), F32),
        compiler_params=pltpu.CompilerParams(
            dimension_semantics=("arbitrary",), vmem_limit_bytes=VMEM_LIMIT),
        name="norm_proj",
    )(x, norm_w.reshape(1, d), w)


def _norm_proj_t_kernel(x_ref, nw_ref, wm_ref, ws_ref, o_ref, *, n_main):
    h = _rmsnorm(x_ref[...], nw_ref[...]).astype(BF16)
    nt = (((1,), (1,)), ((), ()))
    o_ref[:, :n_main] = lax.dot_general(h, wm_ref[...].astype(BF16), nt,
                                        preferred_element_type=F32)
    ws = ws_ref[...]
    ws = jnp.concatenate([ws, jnp.zeros((C_SMALL - ws.shape[0], ws.shape[1]), F32)], axis=0)
    o_ref[:, n_main:] = lax.dot_general(h, ws.astype(BF16), nt, preferred_element_type=F32)


def _norm_proj_t(x, norm_w, wt, i_w, *, tm=512):
    t, d = x.shape
    n = wt.shape[1]
    n_main = (n // LANES) * LANES
    n_small = n - n_main
    assert t % tm == 0 and wt.shape[2] == d, (x.shape, wt.shape, tm)
    assert 0 < n_small <= C_SMALL and n_small % 8 == 0 and n_main % n_small == 0, n
    return pl.pallas_call(
        functools.partial(_norm_proj_t_kernel, n_main=n_main),
        grid=(t // tm,),
        in_specs=[
            pl.BlockSpec((tm, d), lambda i: (i, 0)),
            pl.BlockSpec((1, d), lambda i: (0, 0)),
            _resident((None, n_main, d), (i_w, 0, 0)),
            _resident((None, n_small, d), (i_w, n_main // n_small, 0)),
        ],
        out_specs=pl.BlockSpec((tm, n_main + C_SMALL), lambda i: (i, 0)),
        out_shape=jax.ShapeDtypeStruct((t, n_main + C_SMALL), F32),
        compiler_params=pltpu.CompilerParams(
            dimension_semantics=("arbitrary",), vmem_limit_bytes=VMEM_LIMIT),
        name="norm_proj_t",
    )(x, norm_w.reshape(1, d), wt, wt)


def _stack(x, m0, m1):
    return jnp.concatenate([x * m0, x * m1], axis=0)


def _unstack(y):
    return y[:CHUNK] + y[CHUNK:]


def _mixer_ab_kernel(z_ref, cw_ref, cb_ref, wa_ref, ba_ref, wx_ref, bx_ref, ap_ref, mu_ref,
                     w0_ref, w2_ref, a0_ref, a2_ref, g2_ref, kk_ref, ka_ref, rk_ref,
                     lnw_ref, lnb_ref, o_ref, zprev_ref, hlru_ref, state_ref):
    n_batch = z_ref.shape[0]
    c = CHUNK
    L = LRU_WIDTH

    @pl.when(pl.program_id(0) == 0)
    def _():
        zprev_ref[...] = jnp.zeros_like(zprev_ref)
        hlru_ref[...] = jnp.zeros_like(hlru_ref)
        state_ref[...] = jnp.zeros_like(state_ref)

    lane = _iota((1, LANES), 1)
    m0 = jnp.where(lane < RWKV_HEAD_DIM, 1.0, 0.0)
    m1 = 1.0 - m0
    ri = _iota((c, c), 0)
    ci = _iota((c, c), 1)
    l_incl = jnp.where(ri >= ci, 1.0, 0.0)
    si = _iota((LANES, LANES), 0)
    sj = _iota((LANES, LANES), 1)
    same_head = (si // RWKV_HEAD_DIM) == (sj // RWKV_HEAD_DIM)
    ones_bd = jnp.where(same_head, 1.0, 0.0)
    strict_bd = same_head & (si > sj)
    incl_bd = same_head & (si >= sj)
    inv_n = 1.0 / RWKV_HEAD_DIM

    n_sub = z_ref.shape[1] // c
    units = [(j, b) for j in range(n_sub) for b in range(n_batch)]
    nu = range(len(units))
    groups = [(u, p) for u in nu for p in range(N_PAIRS)]
    tile = lambda x, p: x[:, p * LANES:(p + 1) * LANES]
    rows = lambda j: slice(j * c, (j + 1) * c)
    z = [z_ref[b, rows(j), :] for j, b in units]
    zt = [zprev_ref[b] if j == 0 else z_ref[b, j * c - TAIL:j * c, :] for j, b in units]

    xc = [_causal_conv(z[u][:, :L], zt[u][:, :L], cw_ref) + cb_ref[...] for u in nu]
    xcb = [x.astype(BF16) for x in xc]
    sp_a = _softplus(-ap_ref[...])
    r_gate = [_sigmoid(_dot(tile(xcb[u], p), wa_ref[p]) + tile(ba_ref, p)) for u, p in groups]
    i_gate = [_sigmoid(_dot(tile(xcb[u], p), wx_ref[p]) + tile(bx_ref, p)) for u, p in groups]
    log_a = [-LRU_C * rg_ * tile(sp_a, p) for rg_, (u, p) in zip(r_gate, groups)]
    a = [jnp.exp(x) for x in log_a]
    u_in = [_sqrt(jnp.tanh(-la) * (a_ * a_ + 1.0)) * (ig * tile(xc[u], p))
            for la, a_, ig, (u, p) in zip(log_a, a, i_gate, groups)]
    scans = [_linear_scan(a_, u_) for a_, u_ in zip(a, u_in)]
    gelu = [_gelu_tanh(z[u][:, L + p * LANES:L + (p + 1) * LANES]) for u, p in groups]
    for (a_cum, h), gl_, (u, p) in zip(scans, gelu, groups):
        j, b = units[u]
        sl = slice(p * LANES, (p + 1) * LANES)
        h = h + a_cum * hlru_ref[b, 0:1, sl]
        hlru_ref[b, 0:1, sl] = h[c - 1:c, :]
        o_ref[b, rows(j), sl] = h * gl_

    R = RWKV_WIDTH
    s = [x[:, 2 * L:] for x in z]
    pmix = [s[u] + mu_ref[...] * (_shift_rows(_with_tail(s[u], zt[u][:, 2 * L:]), 1) - s[u])
            for u in nu]
    r = [x[:, :R] for x in pmix]
    k = [x[:, R:2 * R] for x in pmix]
    v = [x[:, 2 * R:3 * R] for x in pmix]
    wa_in = [x[:, 3 * R:3 * R + LANES] for x in pmix]
    gl_in = [x[:, 3 * R + LANES:] for x in pmix]
    w_log = [-_softplus(-(w0_ref[...] + _dot(jnp.tanh(x).astype(BF16), w2_ref[...]))) - 0.5
             for x in wa_in]
    lw = [-jnp.exp(x) for x in w_log]
    a_lr = [_sigmoid(a0_ref[...] + _dot(x.astype(BF16), a2_ref[...])) for x in wa_in]
    g_out = [_dot(_sigmoid(x).astype(BF16), g2_ref[...]) for x in gl_in]
    kkr = [x * kk_ref[...] for x in k]
    k_mod = [kx * (1.0 + (al - 1.0) * ka_ref[...]) for kx, al in zip(k, a_lr)]
    cs = [_cumsum_rows(l_incl, x) for x in lw]
    cl = [x[c - 1:c, :] for x in cs]
    g_in = [jnp.exp(x) for x in cs]
    g_ex = [jnp.exp(x - y) for x, y in zip(cs, lw)]
    g_inv = [jnp.exp(-x) for x in cs]
    g_tot = [jnp.exp(x) for x in cl]
    g_end = [x * y for x, y in zip(g_tot, g_inv)]

    G = lambda xs: [tile(xs[u], p) for u, p in groups]
    rp, vp, ktp, kn, alp = G(r), G(v), G(k_mod), G(kkr), G(a_lr)
    gin, gex, ginv, gend, gtot = G(g_in), G(g_ex), G(g_inv), G(g_end), G(g_tot)
    n2 = [_head_sum(x * x, ones_bd) for x in kn]
    kkp = [x * jnp.minimum(lax.rsqrt(n), 1e12) for x, n in zip(kn, n2)]
    bp = [x * y for x, y in zip(kkp, alp)]
    stk = lambda x: _stack(x, m0, m1)
    kg_st = [stk(x * y) for x, y in zip(kkp, gex)]
    rg = [x * y for x, y in zip(rp, gin)]
    bi_st = [stk(x * y) for x, y in zip(bp, ginv)]
    ki_st = [stk(x * y) for x, y in zip(ktp, ginv)]
    kd_st = [stk(x * y) for x, y in zip(ktp, gend)]
    bd_st = [stk(x * y) for x, y in zip(bp, gend)]
    v_st = [stk(x) for x in vp]
    m = [_dot_nt(jnp.concatenate([kg, stk(rg_)], axis=0), jnp.concatenate([bi, ki], axis=0))
         for kg, rg_, bi, ki in zip(kg_st, rg, bi_st, ki_st)]
    a_m = [jnp.where(strict_bd, x[:2 * c, :2 * c], 0.0) for x in m]
    b_m = [jnp.where(strict_bd, x[:2 * c, 2 * c:], 0.0) for x in m]
    q_m = [jnp.where(incl_bd, x[2 * c:, :2 * c], 0.0) for x in m]
    p_m = [jnp.where(incl_bd, x[2 * c:, 2 * c:], 0.0) for x in m]
    t_inv = _unit_lower_inverse_many(a_m)
    bv = [_dot_hi(x, y) for x, y in zip(b_m, v_st)]
    wu = [_dot_hi(t, jnp.concatenate([kg, x], axis=1)) for t, kg, x in zip(t_inv, kg_st, bv)]
    pv = [_dot_hi(x, y) for x, y in zip(p_m, v_st)]
    vkd = [_dot_tn(x, y) for x, y in zip(v_st, kd_st)]
    bonus = [_head_sum(r_ * kt * tile(rk_ref, p), ones_bd) * v_
             for r_, kt, v_, (u, p) in zip(rp, ktp, vp, groups)]

    for j in range(n_sub):
        idx = [i for i, (u, p) in enumerate(groups) if units[u][0] == j]
        bp_ = [(units[groups[i][0]][1], groups[i][1]) for i in idx]
        st = [state_ref[g] for g in bp_]
        u_st = [_dot_nt(wu[i][:, :LANES], s_) + wu[i][:, LANES:] for i, s_ in zip(idx, st)]
        rs = [_dot_nt(rg[i], s_) for i, s_ in zip(idx, st)]
        qu = [_dot_hi(q_m[i], x) for i, x in zip(idx, u_st)]
        ubd = [_dot_tn(x, bd_st[i]) for i, x in zip(idx, u_st)]
        y = [x + _unstack(pv[i] - q_) for i, x, q_ in zip(idx, rs, qu)]
        for n, (i, g) in enumerate(zip(idx, bp_)):
            state_ref[g] = st[n] * gtot[i] + vkd[i] - ubd[n]
        mean = [_head_sum(x, ones_bd) * inv_n for x in y]
        d = [x - mu_ for x, mu_ in zip(y, mean)]
        var = [_head_sum(x * x, ones_bd) * inv_n for x in d]
        for n, (i, (b, p)) in enumerate(zip(idx, bp_)):
            yn = d[n] * lax.rsqrt(var[n] + RWKV_GN_EPS) * tile(lnw_ref, p) + tile(lnb_ref, p)
            o_ref[b, rows(j), L + p * LANES:L + (p + 1) * LANES] = (
                (yn + bonus[i]) * tile(g_out[groups[i][0]], p))
    for b in range(n_batch):
        zprev_ref[b] = z_ref[b, n_sub * c - TAIL:, :]


def _pair_blockdiag(w):
    g = w.reshape(N_PAIRS, PAIR, LRU_BLOCK, LRU_BLOCK)
    z = jnp.zeros((N_PAIRS, LRU_BLOCK, LRU_BLOCK), w.dtype)
    top = jnp.concatenate([g[:, 0], z], axis=2)
    bot = jnp.concatenate([z, g[:, 1]], axis=2)
    return jnp.concatenate([top, bot], axis=1)


def _mixer_ab(z, conv_w, conv_b, ga_w, ga_b, gx_w, gx_b, a_param, mu, w0, w2, a0, a2, g2,
              k_k, k_a, r_k, ln_w, ln_b):
    b, s, n = z.shape
    rows = CHUNKS_PER_STEP * CHUNK
    assert s % rows == 0 and n == AB_IN, z.shape
    row = lambda t: t.reshape(1, -1)
    w2p = jnp.concatenate([w2, jnp.zeros((ICLR_RANK, RWKV_WIDTH), F32)], axis=0).astype(BF16)
    a2p = jnp.concatenate([jnp.zeros((DECAY_RANK, RWKV_WIDTH), F32), a2], axis=0).astype(BF16)
    args = (z, conv_w, row(conv_b), _pair_blockdiag(ga_w).astype(BF16), row(ga_b),
            _pair_blockdiag(gx_w).astype(BF16), row(gx_b), row(a_param), row(mu), row(w0), w2p,
            row(a0), a2p, g2.astype(BF16), row(k_k), row(k_a), row(r_k), row(ln_w), row(ln_b))

    def full(a):
        nd = a.ndim
        return pl.BlockSpec(a.shape, lambda i: (0,) * nd)

    return pl.pallas_call(
        _mixer_ab_kernel,
        grid=(s // rows,),
        in_specs=[pl.BlockSpec((b, rows, n), lambda i: (0, i, 0))] + [full(a) for a in args[1:]],
        out_specs=pl.BlockSpec((b, rows, D_MODEL), lambda i: (0, i, 0)),
        out_shape=jax.ShapeDtypeStruct((b, s, D_MODEL), F32),
        scratch_shapes=[
            pltpu.VMEM((b, TAIL, n), F32),
            pltpu.VMEM((b, 8, LRU_WIDTH), F32),
            pltpu.VMEM((b, N_PAIRS, LANES, LANES), F32),
        ],
        compiler_params=pltpu.CompilerParams(
            dimension_semantics=("arbitrary",), vmem_limit_bytes=VMEM_LIMIT),
        name="mixer_ab",
    )(*args)


def _mixer_c_kernel(z_ref, cw_ref, alog_ref, dt_ref, nw_ref, o_ref, zprev_ref, state_ref):
    n_batch = z_ref.shape[0]
    c = CHUNK
    W = DN_WIDTH
    dh = DN_HEAD_DIM

    @pl.when(pl.program_id(0) == 0)
    def _():
        zprev_ref[...] = jnp.zeros_like(zprev_ref)
        state_ref[...] = jnp.zeros_like(state_ref)

    ri = _iota((c, c), 0)
    ci = _iota((c, c), 1)
    causal = ri >= ci
    strict = ri > ci
    l_incl = jnp.where(causal, 1.0, 0.0)

    n_sub = z_ref.shape[1] // c
    units = [(j, b) for j in range(n_sub) for b in range(n_batch)]
    groups = [(u, h) for u in range(len(units)) for h in range(DN_HEADS)]
    rows = lambda j: slice(j * c, (j + 1) * c)
    zq = [z_ref[b, rows(j), :3 * W] for j, b in units]
    tail = [zprev_ref[b] if j == 0 else z_ref[b, j * c - TAIL:j * c, :3 * W] for j, b in units]
    qkv = [_silu(_causal_conv(x, t, cw_ref)) for x, t in zip(zq, tail)]
    zs = [z_ref[b, rows(j), 4 * W:] for j, b in units]
    beta_all = [_sigmoid(x) for x in zs]
    g_all = [-jnp.exp(alog_ref[...]) * _softplus(x + dt_ref[...]) for x in zs]
    gc_all = [_cumsum_rows(l_incl, x) for x in g_all]
    gc_rows = [x.T for x in gc_all]

    beta = [beta_all[u][:, h:h + 1] for u, h in groups]
    gcol = [gc_all[u][:, DN_HEADS + h:DN_HEADS + h + 1] for u, h in groups]
    grow = [gc_rows[u][DN_HEADS + h:DN_HEADS + h + 1, :] for u, h in groups]
    glast = [x[c - 1:c, :] for x in gcol]
    q = [qkv[u][:, h * dh:(h + 1) * dh] for u, h in groups]
    k = [qkv[u][:, W + h * dh:W + (h + 1) * dh] for u, h in groups]
    v = [qkv[u][:, 2 * W + h * dh:2 * W + (h + 1) * dh] for u, h in groups]
    ones = jnp.ones((dh, dh), BF16)
    lane_sum = lambda x: jnp.dot(x.astype(BF16), ones, preferred_element_type=F32)
    q = [x * (lax.rsqrt(lane_sum(x * x) + 1e-6) * (dh ** -0.5)) for x in q]
    k = [x * lax.rsqrt(lane_sum(x * x) + 1e-6) for x in k]
    decay = [jnp.exp(jnp.where(causal, gc - gr, -jnp.inf)) for gc, gr in zip(gcol, grow)]
    kk = [_dot_nt(x, x) for x in k]
    a_m = [jnp.where(strict, bt * m * d, 0.0) for bt, m, d in zip(beta, kk, decay)]
    t_inv = _unit_lower_inverse_many(a_m)
    egc = [jnp.exp(x) for x in gcol]
    rhs = [jnp.concatenate([vv * bt, kx * (bt * e)], axis=1)
           for vv, kx, bt, e in zip(v, k, beta, egc)]
    sol = [_dot_hi(t, r) for t, r in zip(t_inv, rhs)]
    attn = [_dot_nt(qx, kx) * d for qx, kx, d in zip(q, k, decay)]
    qe = [qx * e for qx, e in zip(q, egc)]
    kdec = [kx * jnp.exp(gl - gc) for kx, gl, gc in zip(k, glast, gcol)]
    eg = [jnp.exp(x) for x in glast]

    for j in range(n_sub):
        idx = [i for i, (u, h) in enumerate(groups) if units[u][0] == j]
        bh = [(units[groups[i][0]][1], groups[i][1]) for i in idx]
        st = [state_ref[g] for g in bh]
        ws = [_dot_hi(sol[i][:, dh:], s_) for i, s_ in zip(idx, st)]
        v_new = [sol[i][:, :dh] - y for i, y in zip(idx, ws)]
        qs = [_dot_hi(qe[i], s_) for i, s_ in zip(idx, st)]
        av = [_dot_hi(attn[i], vn) for i, vn in zip(idx, v_new)]
        kv = [_dot_tn(kdec[i], vn) for i, vn in zip(idx, v_new)]
        o = [x + y for x, y in zip(qs, av)]
        ms = [lane_sum(x * x) for x in o]
        for n, (i, (b, h)) in enumerate(zip(idx, bh)):
            state_ref[b, h] = st[n] * eg[i] + kv[n]
            gate = z_ref[b, rows(j), 3 * W + h * dh:3 * W + (h + 1) * dh]
            o_ref[b, rows(j), h * dh:(h + 1) * dh] = o[n] * lax.rsqrt(
                ms[n] * (1.0 / dh) + NORM_EPS) * (nw_ref[...] * _silu(gate))
    for b in range(n_batch):
        zprev_ref[b] = z_ref[b, n_sub * c - TAIL:, :3 * W]


def _mixer_c(z, conv_w, a_log, dt_bias, norm_w):
    b, s, n = z.shape
    rows = CHUNKS_PER_STEP * CHUNK
    assert s % rows == 0 and n == C_IN_PAD, z.shape
    pad = jnp.zeros((C_SMALL - 2 * DN_HEADS,), F32)
    lead = jnp.zeros((DN_HEADS,), F32)
    alog_row = jnp.concatenate([lead, a_log, pad]).reshape(1, C_SMALL)
    dt_row = jnp.concatenate([lead, dt_bias, pad]).reshape(1, C_SMALL)
    args = (z, conv_w, alog_row, dt_row, norm_w.reshape(1, DN_HEAD_DIM))

    def full(a):
        nd = a.ndim
        return pl.BlockSpec(a.shape, lambda i: (0,) * nd)

    return pl.pallas_call(
        _mixer_c_kernel,
        grid=(s // rows,),
        in_specs=[pl.BlockSpec((b, rows, n), lambda i: (0, i, 0))] + [full(a) for a in args[1:]],
        out_specs=pl.BlockSpec((b, rows, DN_WIDTH), lambda i: (0, i, 0)),
        out_shape=jax.ShapeDtypeStruct((b, s, DN_WIDTH), F32),
        scratch_shapes=[
            pltpu.VMEM((b, TAIL, 3 * DN_WIDTH), F32),
            pltpu.VMEM((b, DN_HEADS, DN_HEAD_DIM, DN_HEAD_DIM), F32),
        ],
        compiler_params=pltpu.CompilerParams(
            dimension_semantics=("arbitrary",), vmem_limit_bytes=VMEM_LIMIT),
        name="mixer_c",
    )(*args)


def kernel(x, ffn_norm, ffn_w_gate, ffn_w_up, ffn_w_down, mix_norm, final_norm, ab_w_in, lru_conv_w, lru_conv_b, lru_gate_a_w, lru_gate_a_b, lru_gate_x_w, lru_gate_x_b, lru_a_param, rwkv_mu, rwkv_w0, rwkv_w2, rwkv_a0, rwkv_a2, rwkv_g2, rwkv_k_k, rwkv_k_a, rwkv_r_k, rwkv_ln_w, rwkv_ln_b, ab_w_out, c_w_in, dn_conv_w, dn_A_log, dn_dt_bias, dn_norm, c_w_out):
    b, s, d = x.shape
    depth = ffn_norm.shape[0]
    ffn_w = (ffn_w_gate, ffn_w_up, ffn_w_down)
    xf = x.reshape(b * s, d)
    for l in range(depth):
        i = l // 2
        xf = _ffn(xf, ffn_norm[l, 0], *ffn_w, (l, 0), final_norm, final_norm=False)
        if l % 2 == 0:
            z = _norm_proj(xf, mix_norm[l], ab_w_in, i).reshape(b, s, AB_IN)
            y = _mixer_ab(z, lru_conv_w[i], lru_conv_b[i], lru_gate_a_w[i], lru_gate_a_b[i],
                          lru_gate_x_w[i], lru_gate_x_b[i], lru_a_param[i], rwkv_mu[i],
                          rwkv_w0[i], rwkv_w2[i], rwkv_a0[i], rwkv_a2[i], rwkv_g2[i],
                          rwkv_k_k[i], rwkv_k_a[i], rwkv_r_k[i], rwkv_ln_w[i], rwkv_ln_b[i])
            proj = (y.reshape(b * s, D_MODEL), ab_w_out, i)
        else:
            z = _norm_proj_t(xf, mix_norm[l], jnp.swapaxes(c_w_in, 1, 2), i).reshape(b, s, C_IN_PAD)
            y = _mixer_c(z, dn_conv_w[i], dn_A_log[i], dn_dt_bias[i], dn_norm[i])
            proj = (y.reshape(b * s, DN_WIDTH), c_w_out, i)
        xf = _ffn(xf, ffn_norm[l, 1], *ffn_w, (l, 1), final_norm,
                  final_norm=(l == depth - 1), proj=proj)
    return xf.reshape(b, s, d)
```

```python
import functools

import jax
import jax.numpy as jnp
from jax import lax
from jax.experimental import pallas as pl
from jax.experimental.pallas import tpu as pltpu

F32 = jnp.float32
BF16 = jnp.bfloat16

D_MODEL = 1024
D_FF = 2816
NORM_EPS = 1e-6
CONV_WIDTH = 4

LRU_WIDTH = 512
LRU_BLOCK = 64
LRU_C = 8.0

RWKV_WIDTH = 512
RWKV_HEAD_DIM = 64
DECAY_RANK = 64
ICLR_RANK = 64
GATE_RANK = 128
RWKV_GN_EPS = 64e-5
SHIFT_WIDTH = 3 * RWKV_WIDTH + DECAY_RANK + ICLR_RANK + GATE_RANK
AB_IN = 2 * LRU_WIDTH + SHIFT_WIDTH

DN_HEADS = 8
DN_HEAD_DIM = 128
DN_WIDTH = DN_HEADS * DN_HEAD_DIM
C_SMALL = 128
C_IN_PAD = 4 * DN_WIDTH + C_SMALL

LANES = 128
CHUNK = 64
CHUNKS_PER_STEP = 4
TAIL = 8
PAIR = LANES // RWKV_HEAD_DIM
N_PAIRS = RWKV_WIDTH // LANES
INV_LEVELS = 6
VMEM_LIMIT = 56 * 1024 * 1024


def _dot(a, b):
    return jnp.dot(a, b, preferred_element_type=F32)


def _rec_dot(a, b, contract):
    return lax.dot_general(a.astype(BF16), b.astype(BF16), (contract, ((), ())),
                           preferred_element_type=F32)


def _dot_hi(a, b):
    return _rec_dot(a, b, ((1,), (0,)))


def _dot_nt(a, b):
    return _rec_dot(a, b, ((1,), (1,)))


def _dot_tn(a, b):
    return _rec_dot(a, b, ((0,), (0,)))


def _cumsum_rows(l_incl, x):
    lb = l_incl.astype(BF16)
    hi = x.astype(BF16)
    lo = (x - hi.astype(F32)).astype(BF16)
    return _dot(lb, hi) + _dot(lb, lo)


def _head_sum(x, ones_bd):
    return jnp.dot(x.astype(BF16), ones_bd.astype(BF16), preferred_element_type=F32)


def _rmsnorm(x, w, eps=NORM_EPS):
    return x * lax.rsqrt(jnp.mean(x * x, axis=-1, keepdims=True) + eps) * w


def _sigmoid(x):
    return 0.5 + 0.5 * jnp.tanh(0.5 * x)


def _silu(x):
    h = 0.5 * x
    return h + h * jnp.tanh(h)


def _softplus(x):
    return jnp.maximum(x, 0.0) + jnp.log(1.0 + jnp.exp(-jnp.abs(x)))


def _sqrt(x):
    return jnp.where(x > 0.0, x * lax.rsqrt(x), 0.0)


def _gelu_tanh(x):
    return 0.5 * x * (1.0 + jnp.tanh(0.7978845608028654 * (x + 0.044715 * x * x * x)))


def _iota(shape, dim):
    return lax.broadcasted_iota(jnp.int32, shape, dim)


def _with_tail(x, tail):
    return jnp.concatenate([tail, x], axis=0)


def _shift_rows(xt, k):
    return pltpu.roll(xt, k, 0)[TAIL:]


def _causal_conv(x, tail, w_ref):
    xt = _with_tail(x, tail)
    y = w_ref[CONV_WIDTH - 1:CONV_WIDTH, :] * x
    for k in range(1, CONV_WIDTH):
        y = y + w_ref[CONV_WIDTH - 1 - k:CONV_WIDTH - k, :] * _shift_rows(xt, k)
    return y


def _unit_lower_inverse_many(mats):
    eye = jnp.where(_iota(mats[0].shape, 0) == _iota(mats[0].shape, 1), 1.0, 0.0)
    inv = [eye - a for a in mats]
    p = mats
    for _ in range(INV_LEVELS - 1):
        p = [_dot_hi(x, x) for x in p]
        inv = [i + _dot_hi(i, x) for i, x in zip(inv, p)]
    return inv


def _linear_scan(a, b):
    sub = _iota((TAIL, 1), 0)
    tiles = [(a[t:t + TAIL], b[t:t + TAIL]) for t in range(0, a.shape[0], TAIL)]
    k = 1
    while k < TAIL:
        keep = sub >= k
        tiles = [(at * jnp.where(keep, pltpu.roll(at, k, 0), 1.0),
                  at * jnp.where(keep, pltpu.roll(bt, k, 0), 0.0) + bt) for at, bt in tiles]
        k *= 2
    out = [tiles[0]]
    for at, bt in tiles[1:]:
        ca, cb = out[-1]
        out.append((at * ca[TAIL - 1:], bt + at * cb[TAIL - 1:]))
    return jnp.concatenate([x for x, _ in out], axis=0), jnp.concatenate([y for _, y in out], axis=0)


def _resident(block_shape, index):
    return pl.BlockSpec(block_shape, lambda i: index, pipeline_mode=pl.Buffered(1))


def _ffn_kernel(*refs, f_chunk, final_norm, fuse_proj):
    if fuse_proj:
        x_ref, y_ref, wo_ref, nw_ref, wg_ref, wu_ref, wd_ref, fw_ref, o_ref = refs
        x = x_ref[...] + _dot(y_ref[...].astype(BF16), wo_ref[...].astype(BF16))
    else:
        x_ref, nw_ref, wg_ref, wu_ref, wd_ref, fw_ref, o_ref = refs
        x = x_ref[...]
    h = _rmsnorm(x, nw_ref[...]).astype(BF16)
    acc = jnp.zeros_like(x)
    for j in range(D_FF // f_chunk):
        lo = j * f_chunk
        g = _dot(h, wg_ref[:, lo:lo + f_chunk].astype(BF16))
        u = _dot(h, wu_ref[:, lo:lo + f_chunk].astype(BF16))
        acc = acc + _dot((_silu(g) * u).astype(BF16), wd_ref[lo:lo + f_chunk, :].astype(BF16))
    y = x + 0.5 * acc
    if final_norm:
        y = _rmsnorm(y, fw_ref[...])
    o_ref[...] = y


def _ffn(x, norm_w, w_gate, w_up, w_down, lj, final_w, *, final_norm, proj=None,
         tm=512, f_chunk=256):
    t, d = x.shape
    assert t % tm == 0 and D_FF % f_chunk == 0 and w_gate.shape[2:] == (d, D_FF), (x.shape, tm)
    rows = lambda i: (i, 0)
    row_spec = pl.BlockSpec((tm, d), rows)
    vec_spec = pl.BlockSpec((1, d), lambda i: (0, 0))
    in_specs, args = [row_spec], [x]
    if proj is not None:
        y, w_out, i_out = proj
        k = y.shape[1]
        in_specs += [pl.BlockSpec((tm, k), rows), _resident((None, k, d), (i_out, 0, 0))]
        args += [y, w_out]
    in_specs += [vec_spec,
                 _resident((None, None, d, D_FF), lj + (0, 0)),
                 _resident((None, None, d, D_FF), lj + (0, 0)),
                 _resident((None, None, D_FF, d), lj + (0, 0)),
                 vec_spec]
    args += [norm_w.reshape(1, d), w_gate, w_up, w_down, final_w.reshape(1, d)]
    return pl.pallas_call(
        functools.partial(_ffn_kernel, f_chunk=f_chunk, final_norm=final_norm,
                          fuse_proj=proj is not None),
        grid=(t // tm,),
        in_specs=in_specs,
        out_specs=row_spec,
        out_shape=jax.ShapeDtypeStruct((t, d), F32),
        compiler_params=pltpu.CompilerParams(
            dimension_semantics=("arbitrary",), vmem_limit_bytes=VMEM_LIMIT),
        name="ffn_proj" if proj is not None else "ffn",
    )(*args)


def _norm_proj_kernel(x_ref, nw_ref, w_ref, o_ref):
    h = _rmsnorm(x_ref[...], nw_ref[...]).astype(BF16)
    o_ref[...] = _dot(h, w_ref[...].astype(BF16))


def _norm_proj(x, norm_w, w, i_w, *, tm=1024):
    t, d = x.shape
    n = w.shape[2]
    assert t % tm == 0 and n % LANES == 0 and w.shape[1] == d, (x.shape, w.shape, tm)
    return pl.pallas_call(
        _norm_proj_kernel,
        grid=(t // tm,),
        in_specs=[
            pl.BlockSpec((tm, d), lambda i: (i, 0)),
            pl.BlockSpec((1, d), lambda i: (0, 0)),
            _resident((None, d, n), (i_w, 0, 0)),
        ],
        out_specs=pl.BlockSpec((tm, n), lambda i: (i, 0)),
        out_shape=jax.ShapeDtypeStruct((t, n), F32),
        compiler_params=pltpu.CompilerParams(
            dimension_semantics=("arbitrary",), vmem_limit_bytes=VMEM_LIMIT),
        name="norm_proj",
    )(x, norm_w.reshape(1, d), w)


def _norm_proj_t_kernel(x_ref, nw_ref, wm_ref, ws_ref, o_ref, *, n_main):
    h = _rmsnorm(x_ref[...], nw_ref[...]).astype(BF16)
    nt = (((1,), (1,)), ((), ()))
    o_ref[:, :n_main] = lax.dot_general(h, wm_ref[...].astype(BF16), nt,
                                        preferred_element_type=F32)
    ws = ws_ref[...]
    ws = jnp.concatenate([ws, jnp.zeros((C_SMALL - ws.shape[0], ws.shape[1]), F32)], axis=0)
    o_ref[:, n_main:] = lax.dot_general(h, ws.astype(BF16), nt, preferred_element_type=F32)


def _norm_proj_t(x, norm_w, wt, i_w, *, tm=512):
    t, d = x.shape
    n = wt.shape[1]
    n_main = (n // LANES) * LANES
    n_small = n - n_main
    assert t % tm == 0 and wt.shape[2] == d, (x.shape, wt.shape, tm)
    assert 0 < n_small <= C_SMALL and n_small % 8 == 0 and n_main % n_small == 0, n
    return pl.pallas_call(
        functools.partial(_norm_proj_t_kernel, n_main=n_main),
        grid=(t // tm,),
        in_specs=[
            pl.BlockSpec((tm, d), lambda i: (i, 0)),
            pl.BlockSpec((1, d), lambda i: (0, 0)),
            _resident((None, n_main, d), (i_w, 0, 0)),
            _resident((None, n_small, d), (i_w, n_main // n_small, 0)),
        ],
        out_specs=pl.BlockSpec((tm, n_main + C_SMALL), lambda i: (i, 0)),
        out_shape=jax.ShapeDtypeStruct((t, n_main + C_SMALL), F32),
        compiler_params=pltpu.CompilerParams(
            dimension_semantics=("arbitrary",), vmem_limit_bytes=VMEM_LIMIT),
        name="norm_proj_t",
    )(x, norm_w.reshape(1, d), wt, wt)


def _stack(x, m0, m1):
    xb = x.astype(BF16)
    return jnp.concatenate([xb * m0, xb * m1], axis=0)


def _unstack(y):
    return y[:CHUNK] + y[CHUNK:]


def _mixer_ab_kernel(z_ref, cw_ref, cb_ref, wa_ref, ba_ref, wx_ref, bx_ref, ap_ref, mu_ref,
                     w0_ref, w2_ref, a0_ref, a2_ref, g2_ref, kk_ref, ka_ref, rk_ref,
                     lnw_ref, lnb_ref, o_ref, zprev_ref, hlru_ref, state_ref):
    n_batch = z_ref.shape[0]
    c = CHUNK
    L = LRU_WIDTH

    @pl.when(pl.program_id(0) == 0)
    def _():
        zprev_ref[...] = jnp.zeros_like(zprev_ref)
        hlru_ref[...] = jnp.zeros_like(hlru_ref)
        state_ref[...] = jnp.zeros_like(state_ref)

    lane = _iota((1, LANES), 1)
    m0 = jnp.where(lane < RWKV_HEAD_DIM, 1.0, 0.0).astype(BF16)
    m1 = (1.0 - m0).astype(BF16)
    ri = _iota((c, c), 0)
    ci = _iota((c, c), 1)
    l_incl = jnp.where(ri >= ci, 1.0, 0.0)
    si = _iota((LANES, LANES), 0)
    sj = _iota((LANES, LANES), 1)
    same_head = (si // RWKV_HEAD_DIM) == (sj // RWKV_HEAD_DIM)
    ones_bd = jnp.where(same_head, 1.0, 0.0)
    strict_bd = same_head & (si > sj)
    incl_bd = same_head & (si >= sj)
    inv_n = 1.0 / RWKV_HEAD_DIM

    n_sub = z_ref.shape[1] // c
    units = [(j, b) for j in range(n_sub) for b in range(n_batch)]
    nu = range(len(units))
    groups = [(u, p) for u in nu for p in range(N_PAIRS)]
    tile = lambda x, p: x[:, p * LANES:(p + 1) * LANES]
    rows = lambda j: slice(j * c, (j + 1) * c)
    z = [z_ref[b, rows(j), :] for j, b in units]
    zt = [zprev_ref[b] if j == 0 else z_ref[b, j * c - TAIL:j * c, :] for j, b in units]

    xc = [_causal_conv(z[u][:, :L], zt[u][:, :L], cw_ref) + cb_ref[...] for u in nu]
    xcb = [x.astype(BF16) for x in xc]
    sp_a = _softplus(-ap_ref[...])
    r_gate = [_sigmoid(_dot(tile(xcb[u], p), wa_ref[p]) + tile(ba_ref, p)) for u, p in groups]
    i_gate = [_sigmoid(_dot(tile(xcb[u], p), wx_ref[p]) + tile(bx_ref, p)) for u, p in groups]
    log_a = [-LRU_C * rg_ * tile(sp_a, p) for rg_, (u, p) in zip(r_gate, groups)]
    a = [jnp.exp(x) for x in log_a]
    u_in = [_sqrt(jnp.tanh(-la) * (a_ * a_ + 1.0)) * (ig * tile(xc[u], p))
            for la, a_, ig, (u, p) in zip(log_a, a, i_gate, groups)]
    scans = [_linear_scan(a_, u_) for a_, u_ in zip(a, u_in)]
    gelu = [_gelu_tanh(z[u][:, L + p * LANES:L + (p + 1) * LANES]) for u, p in groups]
    for (a_cum, h), gl_, (u, p) in zip(scans, gelu, groups):
        j, b = units[u]
        sl = slice(p * LANES, (p + 1) * LANES)
        h = h + a_cum * hlru_ref[b, 0:1, sl]
        hlru_ref[b, 0:1, sl] = h[c - 1:c, :]
        o_ref[b, rows(j), sl] = h * gl_

    R = RWKV_WIDTH
    s = [x[:, 2 * L:] for x in z]
    pmix = [s[u] + mu_ref[...] * (_shift_rows(_with_tail(s[u], zt[u][:, 2 * L:]), 1) - s[u])
            for u in nu]
    r = [x[:, :R] for x in pmix]
    k = [x[:, R:2 * R] for x in pmix]
    v = [x[:, 2 * R:3 * R] for x in pmix]
    wa_in = [x[:, 3 * R:3 * R + LANES] for x in pmix]
    gl_in = [x[:, 3 * R + LANES:] for x in pmix]
    w_log = [-_softplus(-(w0_ref[...] + _dot(jnp.tanh(x).astype(BF16), w2_ref[...]))) - 0.5
             for x in wa_in]
    lw = [-jnp.exp(x) for x in w_log]
    a_lr = [_sigmoid(a0_ref[...] + _dot(x.astype(BF16), a2_ref[...])) for x in wa_in]
    g_out = [_dot(_sigmoid(x).astype(BF16), g2_ref[...]) for x in gl_in]
    kkr = [x * kk_ref[...] for x in k]
    k_mod = [kx * (1.0 + (al - 1.0) * ka_ref[...]) for kx, al in zip(k, a_lr)]
    cs = [_cumsum_rows(l_incl, x) for x in lw]
    cl = [x[c - 1:c, :] for x in cs]
    g_in = [jnp.exp(x) for x in cs]
    g_ex = [jnp.exp(x - y) for x, y in zip(cs, lw)]
    g_inv = [jnp.exp(-x) for x in cs]
    g_tot = [jnp.exp(x) for x in cl]
    g_end = [x * y for x, y in zip(g_tot, g_inv)]

    G = lambda xs: [tile(xs[u], p) for u, p in groups]
    rp, vp, ktp, kn, alp = G(r), G(v), G(k_mod), G(kkr), G(a_lr)
    gin, gex, ginv, gend, gtot = G(g_in), G(g_ex), G(g_inv), G(g_end), G(g_tot)
    n2 = [_head_sum(x * x, ones_bd) for x in kn]
    kkp = [x * jnp.minimum(lax.rsqrt(n), 1e12) for x, n in zip(kn, n2)]
    bp = [x * y for x, y in zip(kkp, alp)]
    stk = lambda x: _stack(x, m0, m1)
    kg_st = [stk(x * y) for x, y in zip(kkp, gex)]
    rg = [x * y for x, y in zip(rp, gin)]
    bi_st = [stk(x * y) for x, y in zip(bp, ginv)]
    ki_st = [stk(x * y) for x, y in zip(ktp, ginv)]
    kd_st = [stk(x * y) for x, y in zip(ktp, gend)]
    bd_st = [stk(x * y) for x, y in zip(bp, gend)]
    v_st = [stk(x) for x in vp]
    m = [_dot_nt(jnp.concatenate([kg, stk(rg_)], axis=0), jnp.concatenate([bi, ki], axis=0))
         for kg, rg_, bi, ki in zip(kg_st, rg, bi_st, ki_st)]
    a_m = [jnp.where(strict_bd, x[:2 * c, :2 * c], 0.0) for x in m]
    b_m = [jnp.where(strict_bd, x[:2 * c, 2 * c:], 0.0) for x in m]
    q_m = [jnp.where(incl_bd, x[2 * c:, :2 * c], 0.0) for x in m]
    p_m = [jnp.where(incl_bd, x[2 * c:, 2 * c:], 0.0) for x in m]
    t_inv = _unit_lower_inverse_many(a_m)
    bv = [_dot_hi(x, y) for x, y in zip(b_m, v_st)]
    wu = [_dot_hi(t, jnp.concatenate([kg, x.astype(BF16)], axis=1))
          for t, kg, x in zip(t_inv, kg_st, bv)]
    pv = [_dot_hi(x, y) for x, y in zip(p_m, v_st)]
    vkd = [_dot_tn(x, y) for x, y in zip(v_st, kd_st)]
    bonus = [_head_sum(r_ * kt * tile(rk_ref, p), ones_bd) * v_
             for r_, kt, v_, (u, p) in zip(rp, ktp, vp, groups)]

    for j in range(n_sub):
        idx = [i for i, (u, p) in enumerate(groups) if units[u][0] == j]
        bp_ = [(units[groups[i][0]][1], groups[i][1]) for i in idx]
        st = [state_ref[g] for g in bp_]
        u_st = [_dot_nt(wu[i][:, :LANES], s_) + wu[i][:, LANES:] for i, s_ in zip(idx, st)]
        rs = [_dot_nt(rg[i], s_) for i, s_ in zip(idx, st)]
        qu = [_dot_hi(q_m[i], x) for i, x in zip(idx, u_st)]
        ubd = [_dot_tn(x, bd_st[i]) for i, x in zip(idx, u_st)]
        y = [x + _unstack(pv[i] - q_) for i, x, q_ in zip(idx, rs, qu)]
        for n, (i, g) in enumerate(zip(idx, bp_)):
            state_ref[g] = st[n] * gtot[i] + vkd[i] - ubd[n]
        mean = [_head_sum(x, ones_bd) * inv_n for x in y]
        d = [x - mu_ for x, mu_ in zip(y, mean)]
        var = [_head_sum(x * x, ones_bd) * inv_n for x in d]
        for n, (i, (b, p)) in enumerate(zip(idx, bp_)):
            yn = d[n] * lax.rsqrt(var[n] + RWKV_GN_EPS) * tile(lnw_ref, p) + tile(lnb_ref, p)
            o_ref[b, rows(j), L + p * LANES:L + (p + 1) * LANES] = (
                (yn + bonus[i]) * tile(g_out[groups[i][0]], p))
    for b in range(n_batch):
        zprev_ref[b] = z_ref[b, n_sub * c - TAIL:, :]


def _pair_blockdiag(w):
    g = w.reshape(N_PAIRS, PAIR, LRU_BLOCK, LRU_BLOCK)
    z = jnp.zeros((N_PAIRS, LRU_BLOCK, LRU_BLOCK), w.dtype)
    top = jnp.concatenate([g[:, 0], z], axis=2)
    bot = jnp.concatenate([z, g[:, 1]], axis=2)
    return jnp.concatenate([top, bot], axis=1)


def _mixer_ab(z, conv_w, conv_b, ga_w, ga_b, gx_w, gx_b, a_param, mu, w0, w2, a0, a2, g2,
              k_k, k_a, r_k, ln_w, ln_b):
    b, s, n = z.shape
    rows = CHUNKS_PER_STEP * CHUNK
    assert s % rows == 0 and n == AB_IN, z.shape
    row = lambda t: t.reshape(1, -1)
    w2p = jnp.concatenate([w2, jnp.zeros((ICLR_RANK, RWKV_WIDTH), F32)], axis=0).astype(BF16)
    a2p = jnp.concatenate([jnp.zeros((DECAY_RANK, RWKV_WIDTH), F32), a2], axis=0).astype(BF16)
    args = (z, conv_w, row(conv_b), _pair_blockdiag(ga_w).astype(BF16), row(ga_b),
            _pair_blockdiag(gx_w).astype(BF16), row(gx_b), row(a_param), row(mu), row(w0), w2p,
            row(a0), a2p, g2.astype(BF16), row(k_k), row(k_a), row(r_k), row(ln_w), row(ln_b))

    def full(a):
        nd = a.ndim
        return pl.BlockSpec(a.shape, lambda i: (0,) * nd)

    return pl.pallas_call(
        _mixer_ab_kernel,
        grid=(s // rows,),
        in_specs=[pl.BlockSpec((b, rows, n), lambda i: (0, i, 0))] + [full(a) for a in args[1:]],
        out_specs=pl.BlockSpec((b, rows, D_MODEL), lambda i: (0, i, 0)),
        out_shape=jax.ShapeDtypeStruct((b, s, D_MODEL), F32),
        scratch_shapes=[
            pltpu.VMEM((b, TAIL, n), F32),
            pltpu.VMEM((b, 8, LRU_WIDTH), F32),
            pltpu.VMEM((b, N_PAIRS, LANES, LANES), F32),
        ],
        compiler_params=pltpu.CompilerParams(
            dimension_semantics=("arbitrary",), vmem_limit_bytes=VMEM_LIMIT),
        name="mixer_ab",
    )(*args)


def _mixer_c_kernel(z_ref, cw_ref, alog_ref, dt_ref, nw_ref, o_ref, zprev_ref, state_ref):
    n_batch = z_ref.shape[0]
    c = CHUNK
    W = DN_WIDTH
    dh = DN_HEAD_DIM

    @pl.when(pl.program_id(0) == 0)
    def _():
        zprev_ref[...] = jnp.zeros_like(zprev_ref)
        state_ref[...] = jnp.zeros_like(state_ref)

    ri = _iota((c, c), 0)
    ci = _iota((c, c), 1)
    causal = ri >= ci
    strict = ri > ci
    l_incl = jnp.where(causal, 1.0, 0.0)

    n_sub = z_ref.shape[1] // c
    units = [(j, b) for j in range(n_sub) for b in range(n_batch)]
    groups = [(u, h) for u in range(len(units)) for h in range(DN_HEADS)]
    rows = lambda j: slice(j * c, (j + 1) * c)
    zq = [z_ref[b, rows(j), :3 * W] for j, b in units]
    tail = [zprev_ref[b] if j == 0 else z_ref[b, j * c - TAIL:j * c, :3 * W] for j, b in units]
    qkv = [_silu(_causal_conv(x, t, cw_ref)) for x, t in zip(zq, tail)]
    zs = [z_ref[b, rows(j), 4 * W:] for j, b in units]
    beta_all = [_sigmoid(x) for x in zs]
    g_all = [-jnp.exp(alog_ref[...]) * _softplus(x + dt_ref[...]) for x in zs]
    gc_all = [_cumsum_rows(l_incl, x) for x in g_all]
    gc_rows = [x.T for x in gc_all]

    beta = [beta_all[u][:, h:h + 1] for u, h in groups]
    gcol = [gc_all[u][:, DN_HEADS + h:DN_HEADS + h + 1] for u, h in groups]
    grow = [gc_rows[u][DN_HEADS + h:DN_HEADS + h + 1, :] for u, h in groups]
    glast = [x[c - 1:c, :] for x in gcol]
    q = [qkv[u][:, h * dh:(h + 1) * dh] for u, h in groups]
    k = [qkv[u][:, W + h * dh:W + (h + 1) * dh] for u, h in groups]
    v = [qkv[u][:, 2 * W + h * dh:2 * W + (h + 1) * dh] for u, h in groups]
    ones = jnp.ones((dh, dh), BF16)
    lane_sum = lambda x: jnp.dot(x.astype(BF16), ones, preferred_element_type=F32)
    q = [x * (lax.rsqrt(lane_sum(x * x) + 1e-6) * (dh ** -0.5)) for x in q]
    k = [x * lax.rsqrt(lane_sum(x * x) + 1e-6) for x in k]
    decay = [jnp.exp(jnp.where(causal, gc - gr, -jnp.inf)) for gc, gr in zip(gcol, grow)]
    kk = [_dot_nt(x, x) for x in k]
    a_m = [jnp.where(strict, bt * m * d, 0.0) for bt, m, d in zip(beta, kk, decay)]
    t_inv = _unit_lower_inverse_many(a_m)
    egc = [jnp.exp(x) for x in gcol]
    rhs = [jnp.concatenate([vv * bt, kx * (bt * e)], axis=1)
           for vv, kx, bt, e in zip(v, k, beta, egc)]
    sol = [_dot_hi(t, r) for t, r in zip(t_inv, rhs)]
    attn = [_dot_nt(qx, kx) * d for qx, kx, d in zip(q, k, decay)]
    qe = [qx * e for qx, e in zip(q, egc)]
    kdec = [kx * jnp.exp(gl - gc) for kx, gl, gc in zip(k, glast, gcol)]
    eg = [jnp.exp(x) for x in glast]

    for j in range(n_sub):
        idx = [i for i, (u, h) in enumerate(groups) if units[u][0] == j]
        bh = [(units[groups[i][0]][1], groups[i][1]) for i in idx]
        st = [state_ref[g] for g in bh]
        ws = [_dot_hi(sol[i][:, dh:], s_) for i, s_ in zip(idx, st)]
        v_new = [sol[i][:, :dh] - y for i, y in zip(idx, ws)]
        qs = [_dot_hi(qe[i], s_) for i, s_ in zip(idx, st)]
        av = [_dot_hi(attn[i], vn) for i, vn in zip(idx, v_new)]
        kv = [_dot_tn(kdec[i], vn) for i, vn in zip(idx, v_new)]
        o = [x + y for x, y in zip(qs, av)]
        ms = [lane_sum(x * x) for x in o]
        for n, (i, (b, h)) in enumerate(zip(idx, bh)):
            state_ref[b, h] = st[n] * eg[i] + kv[n]
            gate = z_ref[b, rows(j), 3 * W + h * dh:3 * W + (h + 1) * dh]
            o_ref[b, rows(j), h * dh:(h + 1) * dh] = o[n] * lax.rsqrt(
                ms[n] * (1.0 / dh) + NORM_EPS) * (nw_ref[...] * _silu(gate))
    for b in range(n_batch):
        zprev_ref[b] = z_ref[b, n_sub * c - TAIL:, :3 * W]


def _mixer_c(z, conv_w, a_log, dt_bias, norm_w):
    b, s, n = z.shape
    rows = CHUNKS_PER_STEP * CHUNK
    assert s % rows == 0 and n == C_IN_PAD, z.shape
    pad = jnp.zeros((C_SMALL - 2 * DN_HEADS,), F32)
    lead = jnp.zeros((DN_HEADS,), F32)
    alog_row = jnp.concatenate([lead, a_log, pad]).reshape(1, C_SMALL)
    dt_row = jnp.concatenate([lead, dt_bias, pad]).reshape(1, C_SMALL)
    args = (z, conv_w, alog_row, dt_row, norm_w.reshape(1, DN_HEAD_DIM))

    def full(a):
        nd = a.ndim
        return pl.BlockSpec(a.shape, lambda i: (0,) * nd)

    return pl.pallas_call(
        _mixer_c_kernel,
        grid=(s // rows,),
        in_specs=[pl.BlockSpec((b, rows, n), lambda i: (0, i, 0))] + [full(a) for a in args[1:]],
        out_specs=pl.BlockSpec((b, rows, DN_WIDTH), lambda i: (0, i, 0)),
        out_shape=jax.ShapeDtypeStruct((b, s, DN_WIDTH), F32),
        scratch_shapes=[
            pltpu.VMEM((b, TAIL, 3 * DN_WIDTH), F32),
            pltpu.VMEM((b, DN_HEADS, DN_HEAD_DIM, DN_HEAD_DIM), F32),
        ],
        compiler_params=pltpu.CompilerParams(
            dimension_semantics=("arbitrary",), vmem_limit_bytes=VMEM_LIMIT),
        name="mixer_c",
    )(*args)


def kernel(x, ffn_norm, ffn_w_gate, ffn_w_up, ffn_w_down, mix_norm, final_norm, ab_w_in, lru_conv_w, lru_conv_b, lru_gate_a_w, lru_gate_a_b, lru_gate_x_w, lru_gate_x_b, lru_a_param, rwkv_mu, rwkv_w0, rwkv_w2, rwkv_a0, rwkv_a2, rwkv_g2, rwkv_k_k, rwkv_k_a, rwkv_r_k, rwkv_ln_w, rwkv_ln_b, ab_w_out, c_w_in, dn_conv_w, dn_A_log, dn_dt_bias, dn_norm, c_w_out):
    b, s, d = x.shape
    depth = ffn_norm.shape[0]
    ffn_w = (ffn_w_gate, ffn_w_up, ffn_w_down)
    xf = x.reshape(b * s, d)
    for l in range(depth):
        i = l // 2
        xf = _ffn(xf, ffn_norm[l, 0], *ffn_w, (l, 0), final_norm, final_norm=False)
        if l % 2 == 0:
            z = _norm_proj(xf, mix_norm[l], ab_w_in, i).reshape(b, s, AB_IN)
            y = _mixer_ab(z, lru_conv_w[i], lru_conv_b[i], lru_gate_a_w[i], lru_gate_a_b[i],
                          lru_gate_x_w[i], lru_gate_x_b[i], lru_a_param[i], rwkv_mu[i],
                          rwkv_w0[i], rwkv_w2[i], rwkv_a0[i], rwkv_a2[i], rwkv_g2[i],
                          rwkv_k_k[i], rwkv_k_a[i], rwkv_r_k[i], rwkv_ln_w[i], rwkv_ln_b[i])
            proj = (y.reshape(b * s, D_MODEL), ab_w_out, i)
        else:
            z = _norm_proj_t(xf, mix_norm[l], jnp.swapaxes(c_w_in, 1, 2), i).reshape(b, s, C_IN_PAD)
            y = _mixer_c(z, dn_conv_w[i], dn_A_log[i], dn_dt_bias[i], dn_norm[i])
            proj = (y.reshape(b * s, DN_WIDTH), c_w_out, i)
        xf = _ffn(xf, ffn_norm[l, 1], *ffn_w, (l, 1), final_norm,
                  final_norm=(l == depth - 1), proj=proj)
    return xf.reshape(b, s, d)
```

```python
import functools

import jax
import jax.numpy as jnp
from jax import lax
from jax.experimental import pallas as pl
from jax.experimental.pallas import tpu as pltpu

F32 = jnp.float32
BF16 = jnp.bfloat16

D_MODEL = 1024
D_FF = 2816
NORM_EPS = 1e-6
CONV_WIDTH = 4

LRU_WIDTH = 512
LRU_BLOCK = 64
LRU_C = 8.0

RWKV_WIDTH = 512
RWKV_HEAD_DIM = 64
DECAY_RANK = 64
ICLR_RANK = 64
GATE_RANK = 128
RWKV_GN_EPS = 64e-5
SHIFT_WIDTH = 3 * RWKV_WIDTH + DECAY_RANK + ICLR_RANK + GATE_RANK
AB_IN = 2 * LRU_WIDTH + SHIFT_WIDTH

DN_HEADS = 8
DN_HEAD_DIM = 128
DN_WIDTH = DN_HEADS * DN_HEAD_DIM
C_SMALL = 128
C_IN_PAD = 4 * DN_WIDTH + C_SMALL

LANES = 128
CHUNK = 64
CHUNKS_PER_STEP = 4
TAIL = 8
PAIR = LANES // RWKV_HEAD_DIM
N_PAIRS = RWKV_WIDTH // LANES
INV_LEVELS = 6
VMEM_LIMIT = 56 * 1024 * 1024


def _dot(a, b):
    return jnp.dot(a, b, preferred_element_type=F32)


def _rec_dot(a, b, contract):
    return lax.dot_general(a.astype(BF16), b.astype(BF16), (contract, ((), ())),
                           preferred_element_type=F32)


def _dot_hi(a, b):
    return _rec_dot(a, b, ((1,), (0,)))


def _dot_nt(a, b):
    return _rec_dot(a, b, ((1,), (1,)))


def _dot_tn(a, b):
    return _rec_dot(a, b, ((0,), (0,)))


def _cumsum_rows(l_incl, x):
    lb = l_incl.astype(BF16)
    hi = x.astype(BF16)
    lo = (x - hi.astype(F32)).astype(BF16)
    return _dot(lb, hi) + _dot(lb, lo)


def _head_sum(x, ones_bd):
    return jnp.dot(x.astype(BF16), ones_bd.astype(BF16), preferred_element_type=F32)


def _rmsnorm(x, w, eps=NORM_EPS):
    return x * lax.rsqrt(jnp.mean(x * x, axis=-1, keepdims=True) + eps) * w


def _sigmoid(x):
    return 0.5 + 0.5 * jnp.tanh(0.5 * x)


def _silu_of_half(h):
    return h + h * jnp.tanh(h)


def _silu(x):
    return _silu_of_half(0.5 * x)


def _softplus(x):
    return jnp.maximum(x, 0.0) + jnp.log(1.0 + jnp.exp(-jnp.abs(x)))


def _sqrt(x):
    return jnp.where(x > 0.0, x * lax.rsqrt(x), 0.0)


def _gelu_tanh(x):
    return 0.5 * x * (1.0 + jnp.tanh(0.7978845608028654 * (x + 0.044715 * x * x * x)))


def _iota(shape, dim):
    return lax.broadcasted_iota(jnp.int32, shape, dim)


def _with_tail(x, tail):
    return jnp.concatenate([tail, x], axis=0)


def _shift_rows(xt, k):
    return pltpu.roll(xt, k, 0)[TAIL:]


def _causal_conv(x, tail, w_ref, scale=1.0):
    xt = _with_tail(x, tail)
    w = w_ref[...] * scale
    y = w[CONV_WIDTH - 1:CONV_WIDTH, :] * x
    for k in range(1, CONV_WIDTH):
        y = y + w[CONV_WIDTH - 1 - k:CONV_WIDTH - k, :] * _shift_rows(xt, k)
    return y


def _unit_lower_inverse_many(mats):
    eye = jnp.where(_iota(mats[0].shape, 0) == _iota(mats[0].shape, 1), 1.0, 0.0)
    inv = [eye - a for a in mats]
    p = mats
    for _ in range(INV_LEVELS - 1):
        p = [_dot_hi(x, x) for x in p]
        inv = [i + _dot_hi(i, x) for i, x in zip(inv, p)]
    return inv


def _linear_scan(a, b):
    sub = _iota((TAIL, 1), 0)
    tiles = [(a[t:t + TAIL], b[t:t + TAIL]) for t in range(0, a.shape[0], TAIL)]
    k = 1
    while k < TAIL:
        keep = sub >= k
        tiles = [(at * jnp.where(keep, pltpu.roll(at, k, 0), 1.0),
                  at * jnp.where(keep, pltpu.roll(bt, k, 0), 0.0) + bt) for at, bt in tiles]
        k *= 2
    out = [tiles[0]]
    for at, bt in tiles[1:]:
        ca, cb = out[-1]
        out.append((at * ca[TAIL - 1:], bt + at * cb[TAIL - 1:]))
    return jnp.concatenate([x for x, _ in out], axis=0), jnp.concatenate([y for _, y in out], axis=0)


def _resident(block_shape, index):
    return pl.BlockSpec(block_shape, lambda i: index, pipeline_mode=pl.Buffered(1))


def _ffn_kernel(*refs, f_chunk, final_norm, fuse_proj):
    if fuse_proj:
        x_ref, y_ref, wo_ref, nw_ref, wg_ref, wu_ref, wd_ref, fw_ref, o_ref = refs
        x = x_ref[...] + _dot(y_ref[...].astype(BF16), wo_ref[...].astype(BF16))
    else:
        x_ref, nw_ref, wg_ref, wu_ref, wd_ref, fw_ref, o_ref = refs
        x = x_ref[...]
    h = _rmsnorm(x, nw_ref[...]).astype(BF16)
    acc = jnp.zeros_like(x)
    for j in range(D_FF // f_chunk):
        lo = j * f_chunk
        g = _dot(h, wg_ref[:, lo:lo + f_chunk].astype(BF16))
        u = _dot(h, wu_ref[:, lo:lo + f_chunk].astype(BF16))
        acc = acc + _dot((_silu(g) * u).astype(BF16), wd_ref[lo:lo + f_chunk, :].astype(BF16))
    y = x + 0.5 * acc
    if final_norm:
        y = _rmsnorm(y, fw_ref[...])
    o_ref[...] = y


def _ffn(x, norm_w, w_gate, w_up, w_down, lj, final_w, *, final_norm, proj=None,
         tm=512, f_chunk=256):
    t, d = x.shape
    assert t % tm == 0 and D_FF % f_chunk == 0 and w_gate.shape[2:] == (d, D_FF), (x.shape, tm)
    rows = lambda i: (i, 0)
    row_spec = pl.BlockSpec((tm, d), rows)
    vec_spec = pl.BlockSpec((1, d), lambda i: (0, 0))
    in_specs, args = [row_spec], [x]
    if proj is not None:
        y, w_out, i_out = proj
        k = y.shape[1]
        in_specs += [pl.BlockSpec((tm, k), rows), _resident((None, k, d), (i_out, 0, 0))]
        args += [y, w_out]
    in_specs += [vec_spec,
                 _resident((None, None, d, D_FF), lj + (0, 0)),
                 _resident((None, None, d, D_FF), lj + (0, 0)),
                 _resident((None, None, D_FF, d), lj + (0, 0)),
                 vec_spec]
    args += [norm_w.reshape(1, d), w_gate, w_up, w_down, final_w.reshape(1, d)]
    return pl.pallas_call(
        functools.partial(_ffn_kernel, f_chunk=f_chunk, final_norm=final_norm,
                          fuse_proj=proj is not None),
        grid=(t // tm,),
        in_specs=in_specs,
        out_specs=row_spec,
        out_shape=jax.ShapeDtypeStruct((t, d), F32),
        compiler_params=pltpu.CompilerParams(
            dimension_semantics=("arbitrary",), vmem_limit_bytes=VMEM_LIMIT),
        name="ffn_proj" if proj is not None else "ffn",
    )(*args)


def _norm_proj_kernel(x_ref, nw_ref, w_ref, o_ref):
    h = _rmsnorm(x_ref[...], nw_ref[...]).astype(BF16)
    o_ref[...] = _dot(h, w_ref[...].astype(BF16))


def _norm_proj(x, norm_w, w, i_w, *, tm=1024):
    t, d = x.shape
    n = w.shape[2]
    assert t % tm == 0 and n % LANES == 0 and w.shape[1] == d, (x.shape, w.shape, tm)
    return pl.pallas_call(
        _norm_proj_kernel,
        grid=(t // tm,),
        in_specs=[
            pl.BlockSpec((tm, d), lambda i: (i, 0)),
            pl.BlockSpec((1, d), lambda i: (0, 0)),
            _resident((None, d, n), (i_w, 0, 0)),
        ],
        out_specs=pl.BlockSpec((tm, n), lambda i: (i, 0)),
        out_shape=jax.ShapeDtypeStruct((t, n), F32),
        compiler_params=pltpu.CompilerParams(
            dimension_semantics=("arbitrary",), vmem_limit_bytes=VMEM_LIMIT),
        name="norm_proj",
    )(x, norm_w.reshape(1, d), w)


def _norm_proj_t_kernel(x_ref, nw_ref, wm_ref, ws_ref, o_ref, *, n_main):
    h = _rmsnorm(x_ref[...], nw_ref[...]).astype(BF16)
    nt = (((1,), (1,)), ((), ()))
    o_ref[:, :n_main] = lax.dot_general(h, wm_ref[...].astype(BF16), nt,
                                        preferred_element_type=F32)
    ws = ws_ref[...]
    ws = jnp.concatenate([ws, jnp.zeros((C_SMALL - ws.shape[0], ws.shape[1]), F32)], axis=0)
    o_ref[:, n_main:] = lax.dot_general(h, ws.astype(BF16), nt, preferred_element_type=F32)


def _norm_proj_t(x, norm_w, wt, i_w, *, tm=512):
    t, d = x.shape
    n = wt.shape[1]
    n_main = (n // LANES) * LANES
    n_small = n - n_main
    assert t % tm == 0 and wt.shape[2] == d, (x.shape, wt.shape, tm)
    assert 0 < n_small <= C_SMALL and n_small % 8 == 0 and n_main % n_small == 0, n
    return pl.pallas_call(
        functools.partial(_norm_proj_t_kernel, n_main=n_main),
        grid=(t // tm,),
        in_specs=[
            pl.BlockSpec((tm, d), lambda i: (i, 0)),
            pl.BlockSpec((1, d), lambda i: (0, 0)),
            _resident((None, n_main, d), (i_w, 0, 0)),
            _resident((None, n_small, d), (i_w, n_main // n_small, 0)),
        ],
        out_specs=pl.BlockSpec((tm, n_main + C_SMALL), lambda i: (i, 0)),
        out_shape=jax.ShapeDtypeStruct((t, n_main + C_SMALL), F32),
        compiler_params=pltpu.CompilerParams(
            dimension_semantics=("arbitrary",), vmem_limit_bytes=VMEM_LIMIT),
        name="norm_proj_t",
    )(x, norm_w.reshape(1, d), wt, wt)


def _stack(x, m0, m1):
    xb = x.astype(BF16)
    return jnp.concatenate([xb * m0, xb * m1], axis=0)


def _unstack(y):
    return y[:CHUNK] + y[CHUNK:]


def _mixer_ab_kernel(z_ref, cw_ref, cb_ref, wa_ref, ba_ref, wx_ref, bx_ref, ap_ref, mu_ref,
                     w0_ref, w2_ref, a0_ref, a2_ref, g2_ref, kk_ref, ka_ref, rk_ref,
                     lnw_ref, lnb_ref, o_ref, zprev_ref, hlru_ref, state_ref):
    n_batch = z_ref.shape[0]
    c = CHUNK
    L = LRU_WIDTH

    @pl.when(pl.program_id(0) == 0)
    def _():
        zprev_ref[...] = jnp.zeros_like(zprev_ref)
        hlru_ref[...] = jnp.zeros_like(hlru_ref)
        state_ref[...] = jnp.zeros_like(state_ref)

    lane = _iota((1, LANES), 1)
    m0 = jnp.where(lane < RWKV_HEAD_DIM, 1.0, 0.0).astype(BF16)
    m1 = (1.0 - m0).astype(BF16)
    ri = _iota((c, c), 0)
    ci = _iota((c, c), 1)
    l_incl = jnp.where(ri >= ci, 1.0, 0.0)
    si = _iota((LANES, LANES), 0)
    sj = _iota((LANES, LANES), 1)
    same_head = (si // RWKV_HEAD_DIM) == (sj // RWKV_HEAD_DIM)
    ones_bd = jnp.where(same_head, 1.0, 0.0)
    strict_bd = same_head & (si > sj)
    incl_bd = same_head & (si >= sj)
    inv_n = 1.0 / RWKV_HEAD_DIM

    n_sub = z_ref.shape[1] // c
    units = [(j, b) for j in range(n_sub) for b in range(n_batch)]
    nu = range(len(units))
    groups = [(u, p) for u in nu for p in range(N_PAIRS)]
    tile = lambda x, p: x[:, p * LANES:(p + 1) * LANES]
    rows = lambda j: slice(j * c, (j + 1) * c)
    z = [z_ref[b, rows(j), :] for j, b in units]
    zt = [zprev_ref[b] if j == 0 else z_ref[b, j * c - TAIL:j * c, :] for j, b in units]

    xc = [_causal_conv(z[u][:, :L], zt[u][:, :L], cw_ref) + cb_ref[...] for u in nu]
    xcb = [x.astype(BF16) for x in xc]
    sp_a = _softplus(-ap_ref[...])
    r_gate = [_sigmoid(_dot(tile(xcb[u], p), wa_ref[p]) + tile(ba_ref, p)) for u, p in groups]
    i_gate = [_sigmoid(_dot(tile(xcb[u], p), wx_ref[p]) + tile(bx_ref, p)) for u, p in groups]
    log_a = [-LRU_C * rg_ * tile(sp_a, p) for rg_, (u, p) in zip(r_gate, groups)]
    a = [jnp.exp(x) for x in log_a]
    u_in = [_sqrt(jnp.tanh(-la) * (a_ * a_ + 1.0)) * (ig * tile(xc[u], p))
            for la, a_, ig, (u, p) in zip(log_a, a, i_gate, groups)]
    scans = [_linear_scan(a_, u_) for a_, u_ in zip(a, u_in)]
    gelu = [_gelu_tanh(z[u][:, L + p * LANES:L + (p + 1) * LANES]) for u, p in groups]
    for (a_cum, h), gl_, (u, p) in zip(scans, gelu, groups):
        j, b = units[u]
        sl = slice(p * LANES, (p + 1) * LANES)
        h = h + a_cum * hlru_ref[b, 0:1, sl]
        hlru_ref[b, 0:1, sl] = h[c - 1:c, :]
        o_ref[b, rows(j), sl] = (h * gl_).astype(BF16)

    R = RWKV_WIDTH
    s = [x[:, 2 * L:] for x in z]
    pmix = [s[u] + mu_ref[...] * (_shift_rows(_with_tail(s[u], zt[u][:, 2 * L:]), 1) - s[u])
            for u in nu]
    r = [x[:, :R] for x in pmix]
    k = [x[:, R:2 * R] for x in pmix]
    v = [x[:, 2 * R:3 * R] for x in pmix]
    wa_in = [x[:, 3 * R:3 * R + LANES] for x in pmix]
    gl_in = [x[:, 3 * R + LANES:] for x in pmix]
    w_log = [-_softplus(-(w0_ref[...] + _dot(jnp.tanh(x).astype(BF16), w2_ref[...]))) - 0.5
             for x in wa_in]
    lw = [-jnp.exp(x) for x in w_log]
    a_lr = [_sigmoid(a0_ref[...] + _dot(x.astype(BF16), a2_ref[...])) for x in wa_in]
    g_out = [_dot(_sigmoid(x).astype(BF16), g2_ref[...]) for x in gl_in]
    kkr = [x * kk_ref[...] for x in k]
    k_mod = [kx * (1.0 + (al - 1.0) * ka_ref[...]) for kx, al in zip(k, a_lr)]
    cs = [_cumsum_rows(l_incl, x) for x in lw]
    cl = [x[c - 1:c, :] for x in cs]
    g_in = [jnp.exp(x) for x in cs]
    g_ex = [jnp.exp(x - y) for x, y in zip(cs, lw)]
    g_inv = [jnp.exp(-x) for x in cs]
    g_tot = [jnp.exp(x) for x in cl]
    g_end = [x * y for x, y in zip(g_tot, g_inv)]

    G = lambda xs: [tile(xs[u], p) for u, p in groups]
    rp, vp, ktp, kn, alp = G(r), G(v), G(k_mod), G(kkr), G(a_lr)
    gin, gex, ginv, gend, gtot = G(g_in), G(g_ex), G(g_inv), G(g_end), G(g_tot)
    n2 = [_head_sum(x * x, ones_bd) for x in kn]
    kkp = [x * jnp.minimum(lax.rsqrt(n), 1e12) for x, n in zip(kn, n2)]
    bp = [x * y for x, y in zip(kkp, alp)]
    stk = lambda x: _stack(x, m0, m1)
    kg_st = [stk(x * y) for x, y in zip(kkp, gex)]
    rg = [x * y for x, y in zip(rp, gin)]
    bi_st = [stk(x * y) for x, y in zip(bp, ginv)]
    ki_st = [stk(x * y) for x, y in zip(ktp, ginv)]
    kd_st = [stk(x * y) for x, y in zip(ktp, gend)]
    bd_st = [stk(x * y) for x, y in zip(bp, gend)]
    v_st = [stk(x) for x in vp]
    m = [_dot_nt(jnp.concatenate([kg, stk(rg_)], axis=0), jnp.concatenate([bi, ki], axis=0))
         for kg, rg_, bi, ki in zip(kg_st, rg, bi_st, ki_st)]
    a_m = [jnp.where(strict_bd, x[:2 * c, :2 * c], 0.0) for x in m]
    b_m = [jnp.where(strict_bd, x[:2 * c, 2 * c:], 0.0) for x in m]
    q_m = [jnp.where(incl_bd, x[2 * c:, :2 * c], 0.0) for x in m]
    p_m = [jnp.where(incl_bd, x[2 * c:, 2 * c:], 0.0) for x in m]
    t_inv = _unit_lower_inverse_many(a_m)
    bv = [_dot_hi(x, y) for x, y in zip(b_m, v_st)]
    wu = [_dot_hi(t, jnp.concatenate([kg, x.astype(BF16)], axis=1))
          for t, kg, x in zip(t_inv, kg_st, bv)]
    pv = [_dot_hi(x, y) for x, y in zip(p_m, v_st)]
    vkd = [_dot_tn(x, y) for x, y in zip(v_st, kd_st)]
    bonus = [_head_sum(r_ * kt * tile(rk_ref, p), ones_bd) * v_
             for r_, kt, v_, (u, p) in zip(rp, ktp, vp, groups)]

    for j in range(n_sub):
        idx = [i for i, (u, p) in enumerate(groups) if units[u][0] == j]
        bp_ = [(units[groups[i][0]][1], groups[i][1]) for i in idx]
        st = [state_ref[g] for g in bp_]
        u_st = [_dot_nt(wu[i][:, :LANES], s_) + wu[i][:, LANES:] for i, s_ in zip(idx, st)]
        rs = [_dot_nt(rg[i], s_) for i, s_ in zip(idx, st)]
        qu = [_dot_hi(q_m[i], x) for i, x in zip(idx, u_st)]
        ubd = [_dot_tn(x, bd_st[i]) for i, x in zip(idx, u_st)]
        y = [x + _unstack(pv[i] - q_) for i, x, q_ in zip(idx, rs, qu)]
        for n, (i, g) in enumerate(zip(idx, bp_)):
            state_ref[g] = st[n] * gtot[i] + vkd[i] - ubd[n]
        mean = [_head_sum(x, ones_bd) * inv_n for x in y]
        d = [x - mu_ for x, mu_ in zip(y, mean)]
        var = [_head_sum(x * x, ones_bd) * inv_n for x in d]
        for n, (i, (b, p)) in enumerate(zip(idx, bp_)):
            yn = d[n] * lax.rsqrt(var[n] + RWKV_GN_EPS) * tile(lnw_ref, p) + tile(lnb_ref, p)
            o_ref[b, rows(j), L + p * LANES:L + (p + 1) * LANES] = (
                (yn + bonus[i]) * tile(g_out[groups[i][0]], p)).astype(BF16)
    for b in range(n_batch):
        zprev_ref[b] = z_ref[b, n_sub * c - TAIL:, :]


def _pair_blockdiag(w):
    g = w.reshape(N_PAIRS, PAIR, LRU_BLOCK, LRU_BLOCK)
    z = jnp.zeros((N_PAIRS, LRU_BLOCK, LRU_BLOCK), w.dtype)
    top = jnp.concatenate([g[:, 0], z], axis=2)
    bot = jnp.concatenate([z, g[:, 1]], axis=2)
    return jnp.concatenate([top, bot], axis=1)


def _mixer_ab(z, conv_w, conv_b, ga_w, ga_b, gx_w, gx_b, a_param, mu, w0, w2, a0, a2, g2,
              k_k, k_a, r_k, ln_w, ln_b):
    b, s, n = z.shape
    rows = CHUNKS_PER_STEP * CHUNK
    assert s % rows == 0 and n == AB_IN, z.shape
    row = lambda t: t.reshape(1, -1)
    w2p = jnp.concatenate([w2, jnp.zeros((ICLR_RANK, RWKV_WIDTH), F32)], axis=0).astype(BF16)
    a2p = jnp.concatenate([jnp.zeros((DECAY_RANK, RWKV_WIDTH), F32), a2], axis=0).astype(BF16)
    args = (z, conv_w, row(conv_b), _pair_blockdiag(ga_w).astype(BF16), row(ga_b),
            _pair_blockdiag(gx_w).astype(BF16), row(gx_b), row(a_param), row(mu), row(w0), w2p,
            row(a0), a2p, g2.astype(BF16), row(k_k), row(k_a), row(r_k), row(ln_w), row(ln_b))

    def full(a):
        nd = a.ndim
        return pl.BlockSpec(a.shape, lambda i: (0,) * nd)

    return pl.pallas_call(
        _mixer_ab_kernel,
        grid=(s // rows,),
        in_specs=[pl.BlockSpec((b, rows, n), lambda i: (0, i, 0))] + [full(a) for a in args[1:]],
        out_specs=pl.BlockSpec((b, rows, D_MODEL), lambda i: (0, i, 0)),
        out_shape=jax.ShapeDtypeStruct((b, s, D_MODEL), BF16),
        scratch_shapes=[
            pltpu.VMEM((b, TAIL, n), F32),
            pltpu.VMEM((b, TAIL, LRU_WIDTH), F32),
            pltpu.VMEM((b, N_PAIRS, LANES, LANES), F32),
        ],
        compiler_params=pltpu.CompilerParams(
            dimension_semantics=("arbitrary",), vmem_limit_bytes=VMEM_LIMIT),
        name="mixer_ab",
    )(*args)


def _mixer_c_kernel(z_ref, cw_ref, alog_ref, dt_ref, nw_ref, o_ref, zprev_ref, state_ref):
    n_batch = z_ref.shape[0]
    c = CHUNK
    W = DN_WIDTH
    dh = DN_HEAD_DIM

    @pl.when(pl.program_id(0) == 0)
    def _():
        zprev_ref[...] = jnp.zeros_like(zprev_ref)
        state_ref[...] = jnp.zeros_like(state_ref)

    ri = _iota((c, c), 0)
    ci = _iota((c, c), 1)
    causal = ri >= ci
    strict = ri > ci
    l_incl = jnp.where(causal, 1.0, 0.0)

    n_sub = z_ref.shape[1] // c
    units = [(j, b) for j in range(n_sub) for b in range(n_batch)]
    groups = [(u, h) for u in range(len(units)) for h in range(DN_HEADS)]
    rows = lambda j: slice(j * c, (j + 1) * c)
    zq = [z_ref[b, rows(j), :3 * W] for j, b in units]
    tail = [zprev_ref[b] if j == 0 else z_ref[b, j * c - TAIL:j * c, :3 * W] for j, b in units]
    qkv = [_silu_of_half(_causal_conv(x, t, cw_ref, 0.5)) for x, t in zip(zq, tail)]
    zs = [z_ref[b, rows(j), 4 * W:] for j, b in units]
    beta_all = [_sigmoid(x) for x in zs]
    g_all = [-jnp.exp(alog_ref[...]) * _softplus(x + dt_ref[...]) for x in zs]
    gc_all = [_cumsum_rows(l_incl, x) for x in g_all]
    gc_rows = [x.T for x in gc_all]

    beta = [beta_all[u][:, h:h + 1] for u, h in groups]
    gcol = [gc_all[u][:, DN_HEADS + h:DN_HEADS + h + 1] for u, h in groups]
    grow = [gc_rows[u][DN_HEADS + h:DN_HEADS + h + 1, :] for u, h in groups]
    glast = [x[c - 1:c, :] for x in gcol]
    q = [qkv[u][:, h * dh:(h + 1) * dh] for u, h in groups]
    k = [qkv[u][:, W + h * dh:W + (h + 1) * dh] for u, h in groups]
    v = [qkv[u][:, 2 * W + h * dh:2 * W + (h + 1) * dh] for u, h in groups]
    ones = jnp.ones((dh, dh), BF16)
    lane_sum = lambda x: jnp.dot(x.astype(BF16), ones, preferred_element_type=F32)
    q = [x * (lax.rsqrt(lane_sum(x * x) + 1e-6) * (dh ** -0.5)) for x in q]
    k = [x * lax.rsqrt(lane_sum(x * x) + 1e-6) for x in k]
    decay = [jnp.exp(jnp.where(causal, gc - gr, -jnp.inf)) for gc, gr in zip(gcol, grow)]
    kk = [_dot_nt(x, x) for x in k]
    a_m = [jnp.where(strict, bt * m * d, 0.0) for bt, m, d in zip(beta, kk, decay)]
    t_inv = _unit_lower_inverse_many(a_m)
    egc = [jnp.exp(x) for x in gcol]
    rhs = [jnp.concatenate([vv * bt, kx * (bt * e)], axis=1)
           for vv, kx, bt, e in zip(v, k, beta, egc)]
    sol = [_dot_hi(t, r) for t, r in zip(t_inv, rhs)]
    attn = [_dot_nt(qx, kx) * d for qx, kx, d in zip(q, k, decay)]
    qe = [qx * e for qx, e in zip(q, egc)]
    kdec = [kx * jnp.exp(gl - gc) for kx, gl, gc in zip(k, glast, gcol)]
    eg = [jnp.exp(x) for x in glast]

    for j in range(n_sub):
        idx = [i for i, (u, h) in enumerate(groups) if units[u][0] == j]
        bh = [(units[groups[i][0]][1], groups[i][1]) for i in idx]
        st = [state_ref[g] for g in bh]
        ws = [_dot_hi(sol[i][:, dh:], s_) for i, s_ in zip(idx, st)]
        v_new = [sol[i][:, :dh] - y for i, y in zip(idx, ws)]
        qs = [_dot_hi(qe[i], s_) for i, s_ in zip(idx, st)]
        av = [_dot_hi(attn[i], vn) for i, vn in zip(idx, v_new)]
        kv = [_dot_tn(kdec[i], vn) for i, vn in zip(idx, v_new)]
        o = [x + y for x, y in zip(qs, av)]
        ms = [lane_sum(x * x) for x in o]
        for n, (i, (b, h)) in enumerate(zip(idx, bh)):
            state_ref[b, h] = st[n] * eg[i] + kv[n]
            gate = z_ref[b, rows(j), 3 * W + h * dh:3 * W + (h + 1) * dh]
            o_ref[b, rows(j), h * dh:(h + 1) * dh] = (o[n] * lax.rsqrt(
                ms[n] * (1.0 / dh) + NORM_EPS) * (nw_ref[...] * _silu(gate))).astype(BF16)
    for b in range(n_batch):
        zprev_ref[b] = z_ref[b, n_sub * c - TAIL:, :3 * W]


def _mixer_c(z, conv_w, a_log, dt_bias, norm_w):
    b, s, n = z.shape
    rows = CHUNKS_PER_STEP * CHUNK
    assert s % rows == 0 and n == C_IN_PAD, z.shape
    pad = jnp.zeros((C_SMALL - 2 * DN_HEADS,), F32)
    lead = jnp.zeros((DN_HEADS,), F32)
    alog_row = jnp.concatenate([lead, a_log, pad]).reshape(1, C_SMALL)
    dt_row = jnp.concatenate([lead, dt_bias, pad]).reshape(1, C_SMALL)
    args = (z, conv_w, alog_row, dt_row, norm_w.reshape(1, DN_HEAD_DIM))

    def full(a):
        nd = a.ndim
        return pl.BlockSpec(a.shape, lambda i: (0,) * nd)

    return pl.pallas_call(
        _mixer_c_kernel,
        grid=(s // rows,),
        in_specs=[pl.BlockSpec((b, rows, n), lambda i: (0, i, 0))] + [full(a) for a in args[1:]],
        out_specs=pl.BlockSpec((b, rows, DN_WIDTH), lambda i: (0, i, 0)),
        out_shape=jax.ShapeDtypeStruct((b, s, DN_WIDTH), BF16),
        scratch_shapes=[
            pltpu.VMEM((b, TAIL, 3 * DN_WIDTH), F32),
            pltpu.VMEM((b, DN_HEADS, DN_HEAD_DIM, DN_HEAD_DIM), F32),
        ],
        compiler_params=pltpu.CompilerParams(
            dimension_semantics=("arbitrary",), vmem_limit_bytes=VMEM_LIMIT),
        name="mixer_c",
    )(*args)


def kernel(x, ffn_norm, ffn_w_gate, ffn_w_up, ffn_w_down, mix_norm, final_norm, ab_w_in, lru_conv_w, lru_conv_b, lru_gate_a_w, lru_gate_a_b, lru_gate_x_w, lru_gate_x_b, lru_a_param, rwkv_mu, rwkv_w0, rwkv_w2, rwkv_a0, rwkv_a2, rwkv_g2, rwkv_k_k, rwkv_k_a, rwkv_r_k, rwkv_ln_w, rwkv_ln_b, ab_w_out, c_w_in, dn_conv_w, dn_A_log, dn_dt_bias, dn_norm, c_w_out):
    b, s, d = x.shape
    depth = ffn_norm.shape[0]
    ffn_w = (ffn_w_gate, ffn_w_up, ffn_w_down)
    xf = x.reshape(b * s, d)
    for l in range(depth):
        i = l // 2
        xf = _ffn(xf, ffn_norm[l, 0], *ffn_w, (l, 0), final_norm, final_norm=False)
        if l % 2 == 0:
            z = _norm_proj(xf, mix_norm[l], ab_w_in, i).reshape(b, s, AB_IN)
            y = _mixer_ab(z, lru_conv_w[i], lru_conv_b[i], lru_gate_a_w[i], lru_gate_a_b[i],
                          lru_gate_x_w[i], lru_gate_x_b[i], lru_a_param[i], rwkv_mu[i],
                          rwkv_w0[i], rwkv_w2[i], rwkv_a0[i], rwkv_a2[i], rwkv_g2[i],
                          rwkv_k_k[i], rwkv_k_a[i], rwkv_r_k[i], rwkv_ln_w[i], rwkv_ln_b[i])
            proj = (y.reshape(b * s, D_MODEL), ab_w_out, i)
        else:
            z = _norm_proj_t(xf, mix_norm[l], jnp.swapaxes(c_w_in, 1, 2), i).reshape(b, s, C_IN_PAD)
            y = _mixer_c(z, dn_conv_w[i], dn_A_log[i], dn_dt_bias[i], dn_norm[i])
            proj = (y.reshape(b * s, DN_WIDTH), c_w_out, i)
        xf = _ffn(xf, ffn_norm[l, 1], *ffn_w, (l, 1), final_norm,
                  final_norm=(l == depth - 1), proj=proj)
    return xf.reshape(b, s, d)
```

```python
import functools

import jax
import jax.numpy as jnp
from jax import lax
from jax.experimental import pallas as pl
from jax.experimental.pallas import tpu as pltpu

F32 = jnp.float32
BF16 = jnp.bfloat16

D_MODEL = 1024
D_FF = 2816
NORM_EPS = 1e-6
CONV_WIDTH = 4

LRU_WIDTH = 512
LRU_BLOCK = 64
LRU_C = 8.0

RWKV_WIDTH = 512
RWKV_HEAD_DIM = 64
DECAY_RANK = 64
ICLR_RANK = 64
GATE_RANK = 128
RWKV_GN_EPS = 64e-5
SHIFT_WIDTH = 3 * RWKV_WIDTH + DECAY_RANK + ICLR_RANK + GATE_RANK
AB_IN = 2 * LRU_WIDTH + SHIFT_WIDTH

DN_HEADS = 8
DN_HEAD_DIM = 128
DN_WIDTH = DN_HEADS * DN_HEAD_DIM
C_SMALL = 128
C_IN_PAD = 4 * DN_WIDTH + C_SMALL

LANES = 128
CHUNK = 64
CHUNKS_PER_STEP = 4
TAIL = 8
PAIR = LANES // RWKV_HEAD_DIM
N_PAIRS = RWKV_WIDTH // LANES
N_ROW_PARAMS = 11
INV_LEVELS = 6
VMEM_LIMIT = 56 * 1024 * 1024


def _dot(a, b):
    return jnp.dot(a, b, preferred_element_type=F32)


def _rec_dot(a, b, contract):
    return lax.dot_general(a.astype(BF16), b.astype(BF16), (contract, ((), ())),
                           preferred_element_type=F32)


def _dot_hi(a, b):
    return _rec_dot(a, b, ((1,), (0,)))


def _dot_nt(a, b):
    return _rec_dot(a, b, ((1,), (1,)))


def _dot_tn(a, b):
    return _rec_dot(a, b, ((0,), (0,)))


def _cumsum_rows(l_incl, x):
    lb = l_incl.astype(BF16)
    hi = x.astype(BF16)
    lo = (x - hi.astype(F32)).astype(BF16)
    return _dot(lb, hi) + _dot(lb, lo)


def _head_sum(x, ones_bd):
    return jnp.dot(x.astype(BF16), ones_bd.astype(BF16), preferred_element_type=F32)


def _rmsnorm(x, w, eps=NORM_EPS):
    return x * lax.rsqrt(jnp.mean(x * x, axis=-1, keepdims=True) + eps) * w


def _sigmoid(x):
    return 0.5 + 0.5 * jnp.tanh(0.5 * x)


def _silu_of_half(h):
    return h + h * jnp.tanh(h)


def _silu(x):
    return _silu_of_half(0.5 * x)


def _softplus(x):
    return jnp.maximum(x, 0.0) + jnp.log(1.0 + jnp.exp(-jnp.abs(x)))


def _sqrt(x):
    return jnp.where(x > 0.0, x * lax.rsqrt(x), 0.0)


def _gelu_tanh(x):
    return 0.5 * x * (1.0 + jnp.tanh(0.7978845608028654 * (x + 0.044715 * x * x * x)))


def _iota(shape, dim):
    return lax.broadcasted_iota(jnp.int32, shape, dim)


def _with_tail(x, tail):
    return jnp.concatenate([tail, x], axis=0)


def _shift_rows(xt, k):
    return pltpu.roll(xt, k, 0)[TAIL:]


def _causal_conv(x, tail, w_ref, scale=1.0):
    xt = _with_tail(x, tail)
    w = w_ref[...] * scale
    y = w[CONV_WIDTH - 1:CONV_WIDTH, :] * x
    for k in range(1, CONV_WIDTH):
        y = y + w[CONV_WIDTH - 1 - k:CONV_WIDTH - k, :] * _shift_rows(xt, k)
    return y


def _unit_lower_inverse_many(mats):
    eye = jnp.where(_iota(mats[0].shape, 0) == _iota(mats[0].shape, 1), 1.0, 0.0)
    inv = [eye - a for a in mats]
    p = mats
    for _ in range(INV_LEVELS - 1):
        p = [_dot_hi(x, x) for x in p]
        inv = [i + _dot_hi(i, x) for i, x in zip(inv, p)]
    return inv


def _linear_scan(a, b):
    sub = _iota((TAIL, 1), 0)
    tiles = [(a[t:t + TAIL], b[t:t + TAIL]) for t in range(0, a.shape[0], TAIL)]
    k = 1
    while k < TAIL:
        keep = sub >= k
        tiles = [(at * jnp.where(keep, pltpu.roll(at, k, 0), 1.0),
                  at * jnp.where(keep, pltpu.roll(bt, k, 0), 0.0) + bt) for at, bt in tiles]
        k *= 2
    out = [tiles[0]]
    for at, bt in tiles[1:]:
        ca, cb = out[-1]
        out.append((at * ca[TAIL - 1:], bt + at * cb[TAIL - 1:]))
    return jnp.concatenate([x for x, _ in out], axis=0), jnp.concatenate([y for _, y in out], axis=0)


def _resident(block_shape, index):
    return pl.BlockSpec(block_shape, lambda i: index, pipeline_mode=pl.Buffered(1))


def _ffn_kernel(*refs, f_chunk, final_norm, fuse_proj):
    if fuse_proj:
        x_ref, y_ref, wo_ref, nw_ref, wg_ref, wu_ref, wd_ref, fw_ref, o_ref = refs
        x = x_ref[...] + _dot(y_ref[...].astype(BF16), wo_ref[...].astype(BF16))
    else:
        x_ref, nw_ref, wg_ref, wu_ref, wd_ref, fw_ref, o_ref = refs
        x = x_ref[...]
    h = _rmsnorm(x, nw_ref[...]).astype(BF16)
    acc = jnp.zeros_like(x)
    for j in range(D_FF // f_chunk):
        lo = j * f_chunk
        g = _dot(h, wg_ref[:, lo:lo + f_chunk].astype(BF16))
        u = _dot(h, wu_ref[:, lo:lo + f_chunk].astype(BF16))
        acc = acc + _dot((_silu(g) * u).astype(BF16), wd_ref[lo:lo + f_chunk, :].astype(BF16))
    y = x + 0.5 * acc
    if final_norm:
        y = _rmsnorm(y, fw_ref[...])
    o_ref[...] = y


def _ffn(x, norm_w, w_gate, w_up, w_down, lj, final_w, *, final_norm, proj=None,
         tm=512, f_chunk=256):
    t, d = x.shape
    assert t % tm == 0 and D_FF % f_chunk == 0 and w_gate.shape[2:] == (d, D_FF), (x.shape, tm)
    rows = lambda i: (i, 0)
    row_spec = pl.BlockSpec((tm, d), rows)
    vec_spec = pl.BlockSpec((1, d), lambda i: (0, 0))
    in_specs, args = [row_spec], [x]
    if proj is not None:
        y, w_out, i_out = proj
        k = y.shape[1]
        in_specs += [pl.BlockSpec((tm, k), rows), _resident((None, k, d), (i_out, 0, 0))]
        args += [y, w_out]
    in_specs += [vec_spec,
                 _resident((None, None, d, D_FF), lj + (0, 0)),
                 _resident((None, None, d, D_FF), lj + (0, 0)),
                 _resident((None, None, D_FF, d), lj + (0, 0)),
                 vec_spec]
    args += [norm_w.reshape(1, d), w_gate, w_up, w_down, final_w.reshape(1, d)]
    return pl.pallas_call(
        functools.partial(_ffn_kernel, f_chunk=f_chunk, final_norm=final_norm,
                          fuse_proj=proj is not None),
        grid=(t // tm,),
        in_specs=in_specs,
        out_specs=row_spec,
        out_shape=jax.ShapeDtypeStruct((t, d), F32),
        compiler_params=pltpu.CompilerParams(
            dimension_semantics=("arbitrary",), vmem_limit_bytes=VMEM_LIMIT),
        name="ffn_proj" if proj is not None else "ffn",
    )(*args)


def _norm_proj_kernel(x_ref, nw_ref, w_ref, o_ref):
    h = _rmsnorm(x_ref[...], nw_ref[...]).astype(BF16)
    o_ref[...] = _dot(h, w_ref[...].astype(BF16))


def _norm_proj(x, norm_w, w, i_w, *, tm=1024):
    t, d = x.shape
    n = w.shape[2]
    assert t % tm == 0 and n % LANES == 0 and w.shape[1] == d, (x.shape, w.shape, tm)
    return pl.pallas_call(
        _norm_proj_kernel,
        grid=(t // tm,),
        in_specs=[
            pl.BlockSpec((tm, d), lambda i: (i, 0)),
            pl.BlockSpec((1, d), lambda i: (0, 0)),
            _resident((None, d, n), (i_w, 0, 0)),
        ],
        out_specs=pl.BlockSpec((tm, n), lambda i: (i, 0)),
        out_shape=jax.ShapeDtypeStruct((t, n), F32),
        compiler_params=pltpu.CompilerParams(
            dimension_semantics=("arbitrary",), vmem_limit_bytes=VMEM_LIMIT),
        name="norm_proj",
    )(x, norm_w.reshape(1, d), w)


def _norm_proj_t_kernel(x_ref, nw_ref, wm_ref, ws_ref, o_ref, *, n_main):
    h = _rmsnorm(x_ref[...], nw_ref[...]).astype(BF16)
    nt = (((1,), (1,)), ((), ()))
    o_ref[:, :n_main] = lax.dot_general(h, wm_ref[...].astype(BF16), nt,
                                        preferred_element_type=F32)
    ws = ws_ref[...]
    ws = jnp.concatenate([ws, jnp.zeros((C_SMALL - ws.shape[0], ws.shape[1]), F32)], axis=0)
    o_ref[:, n_main:] = lax.dot_general(h, ws.astype(BF16), nt, preferred_element_type=F32)


def _norm_proj_t(x, norm_w, wt, i_w, *, tm=512):
    t, d = x.shape
    n = wt.shape[1]
    n_main = (n // LANES) * LANES
    n_small = n - n_main
    assert t % tm == 0 and wt.shape[2] == d, (x.shape, wt.shape, tm)
    assert 0 < n_small <= C_SMALL and n_small % 8 == 0 and n_main % n_small == 0, n
    return pl.pallas_call(
        functools.partial(_norm_proj_t_kernel, n_main=n_main),
        grid=(t // tm,),
        in_specs=[
            pl.BlockSpec((tm, d), lambda i: (i, 0)),
            pl.BlockSpec((1, d), lambda i: (0, 0)),
            _resident((None, n_main, d), (i_w, 0, 0)),
            _resident((None, n_small, d), (i_w, n_main // n_small, 0)),
        ],
        out_specs=pl.BlockSpec((tm, n_main + C_SMALL), lambda i: (i, 0)),
        out_shape=jax.ShapeDtypeStruct((t, n_main + C_SMALL), F32),
        compiler_params=pltpu.CompilerParams(
            dimension_semantics=("arbitrary",), vmem_limit_bytes=VMEM_LIMIT),
        name="norm_proj_t",
    )(x, norm_w.reshape(1, d), wt, wt)


def _stack(x, m0, m1):
    xb = x.astype(BF16)
    return jnp.concatenate([xb * m0, xb * m1], axis=0)


def _unstack(y):
    return y[:CHUNK] + y[CHUNK:]


def _mixer_ab_kernel(z_ref, rows_ref, mu_ref, gw_ref, lr_ref, o_ref, zprev_ref, hlru_ref,
                     state_ref):
    cw_ref = rows_ref.at[0:CONV_WIDTH, :]
    (cb_ref, ba_ref, bx_ref, ap_ref, w0_ref, a0_ref, kk_ref, ka_ref, rk_ref, lnw_ref,
     lnb_ref) = [rows_ref.at[CONV_WIDTH + r:CONV_WIDTH + r + 1, :] for r in range(N_ROW_PARAMS)]
    wa_ref, wx_ref = gw_ref.at[0:N_PAIRS], gw_ref.at[N_PAIRS:2 * N_PAIRS]
    w2_ref, a2_ref, g2_ref = lr_ref.at[0], lr_ref.at[1], lr_ref.at[2]
    n_batch = z_ref.shape[0]
    c = CHUNK
    L = LRU_WIDTH

    @pl.when(pl.program_id(0) == 0)
    def _():
        zprev_ref[...] = jnp.zeros_like(zprev_ref)
        hlru_ref[...] = jnp.zeros_like(hlru_ref)
        state_ref[...] = jnp.zeros_like(state_ref)

    lane = _iota((1, LANES), 1)
    m0 = jnp.where(lane < RWKV_HEAD_DIM, 1.0, 0.0).astype(BF16)
    m1 = (1.0 - m0).astype(BF16)
    ri = _iota((c, c), 0)
    ci = _iota((c, c), 1)
    l_incl = jnp.where(ri >= ci, 1.0, 0.0)
    si = _iota((LANES, LANES), 0)
    sj = _iota((LANES, LANES), 1)
    same_head = (si // RWKV_HEAD_DIM) == (sj // RWKV_HEAD_DIM)
    ones_bd = jnp.where(same_head, 1.0, 0.0)
    strict_bd = same_head & (si > sj)
    incl_bd = same_head & (si >= sj)
    inv_n = 1.0 / RWKV_HEAD_DIM

    n_sub = z_ref.shape[1] // c
    units = [(j, b) for j in range(n_sub) for b in range(n_batch)]
    nu = range(len(units))
    groups = [(u, p) for u in nu for p in range(N_PAIRS)]
    tile = lambda x, p: x[:, p * LANES:(p + 1) * LANES]
    rows = lambda j: slice(j * c, (j + 1) * c)
    z = [z_ref[b, rows(j), :] for j, b in units]
    zt = [zprev_ref[b] if j == 0 else z_ref[b, j * c - TAIL:j * c, :] for j, b in units]

    xc = [_causal_conv(z[u][:, :L], zt[u][:, :L], cw_ref) + cb_ref[...] for u in nu]
    xcb = [x.astype(BF16) for x in xc]
    sp_a = _softplus(-ap_ref[...])
    r_gate = [_sigmoid(_dot(tile(xcb[u], p), wa_ref[p]) + tile(ba_ref, p)) for u, p in groups]
    i_gate = [_sigmoid(_dot(tile(xcb[u], p), wx_ref[p]) + tile(bx_ref, p)) for u, p in groups]
    log_a = [-LRU_C * rg_ * tile(sp_a, p) for rg_, (u, p) in zip(r_gate, groups)]
    a = [jnp.exp(x) for x in log_a]
    u_in = [_sqrt(jnp.tanh(-la) * (a_ * a_ + 1.0)) * (ig * tile(xc[u], p))
            for la, a_, ig, (u, p) in zip(log_a, a, i_gate, groups)]
    scans = [_linear_scan(a_, u_) for a_, u_ in zip(a, u_in)]
    gelu = [_gelu_tanh(z[u][:, L + p * LANES:L + (p + 1) * LANES]) for u, p in groups]
    for (a_cum, h), gl_, (u, p) in zip(scans, gelu, groups):
        j, b = units[u]
        sl = slice(p * LANES, (p + 1) * LANES)
        h = h + a_cum * hlru_ref[b, 0:1, sl]
        hlru_ref[b, 0:1, sl] = h[c - 1:c, :]
        o_ref[b, rows(j), sl] = (h * gl_).astype(BF16)

    R = RWKV_WIDTH
    s = [x[:, 2 * L:] for x in z]
    pmix = [s[u] + mu_ref[...] * (_shift_rows(_with_tail(s[u], zt[u][:, 2 * L:]), 1) - s[u])
            for u in nu]
    r = [x[:, :R] for x in pmix]
    k = [x[:, R:2 * R] for x in pmix]
    v = [x[:, 2 * R:3 * R] for x in pmix]
    wa_in = [x[:, 3 * R:3 * R + LANES] for x in pmix]
    gl_in = [x[:, 3 * R + LANES:] for x in pmix]
    w_log = [-_softplus(-(w0_ref[...] + _dot(jnp.tanh(x).astype(BF16), w2_ref[...]))) - 0.5
             for x in wa_in]
    lw = [-jnp.exp(x) for x in w_log]
    a_lr = [_sigmoid(a0_ref[...] + _dot(x.astype(BF16), a2_ref[...])) for x in wa_in]
    g_out = [_dot(_sigmoid(x).astype(BF16), g2_ref[...]) for x in gl_in]
    kkr = [x * kk_ref[...] for x in k]
    k_mod = [kx * (1.0 + (al - 1.0) * ka_ref[...]) for kx, al in zip(k, a_lr)]
    cs = [_cumsum_rows(l_incl, x) for x in lw]
    cl = [x[c - 1:c, :] for x in cs]
    g_in = [jnp.exp(x) for x in cs]
    g_ex = [jnp.exp(x - y) for x, y in zip(cs, lw)]
    g_inv = [jnp.exp(-x) for x in cs]
    g_tot = [jnp.exp(x) for x in cl]
    g_end = [x * y for x, y in zip(g_tot, g_inv)]

    G = lambda xs: [tile(xs[u], p) for u, p in groups]
    rp, vp, ktp, kn, alp = G(r), G(v), G(k_mod), G(kkr), G(a_lr)
    gin, gex, ginv, gend, gtot = G(g_in), G(g_ex), G(g_inv), G(g_end), G(g_tot)
    n2 = [_head_sum(x * x, ones_bd) for x in kn]
    kkp = [x * jnp.minimum(lax.rsqrt(n), 1e12) for x, n in zip(kn, n2)]
    bp = [x * y for x, y in zip(kkp, alp)]
    stk = lambda x: _stack(x, m0, m1)
    kg_st = [stk(x * y) for x, y in zip(kkp, gex)]
    rg = [x * y for x, y in zip(rp, gin)]
    bi_st = [stk(x * y) for x, y in zip(bp, ginv)]
    ki_st = [stk(x * y) for x, y in zip(ktp, ginv)]
    kd_st = [stk(x * y) for x, y in zip(ktp, gend)]
    bd_st = [stk(x * y) for x, y in zip(bp, gend)]
    v_st = [stk(x) for x in vp]
    m = [_dot_nt(jnp.concatenate([kg, stk(rg_)], axis=0), jnp.concatenate([bi, ki], axis=0))
         for kg, rg_, bi, ki in zip(kg_st, rg, bi_st, ki_st)]
    a_m = [jnp.where(strict_bd, x[:2 * c, :2 * c], 0.0) for x in m]
    b_m = [jnp.where(strict_bd, x[:2 * c, 2 * c:], 0.0) for x in m]
    q_m = [jnp.where(incl_bd, x[2 * c:, :2 * c], 0.0) for x in m]
    p_m = [jnp.where(incl_bd, x[2 * c:, 2 * c:], 0.0) for x in m]
    t_inv = _unit_lower_inverse_many(a_m)
    bv = [_dot_hi(x, y) for x, y in zip(b_m, v_st)]
    wu = [_dot_hi(t, jnp.concatenate([kg, x.astype(BF16)], axis=1))
          for t, kg, x in zip(t_inv, kg_st, bv)]
    pv = [_dot_hi(x, y) for x, y in zip(p_m, v_st)]
    vkd = [_dot_tn(x, y) for x, y in zip(v_st, kd_st)]
    bonus = [_head_sum(r_ * kt * tile(rk_ref, p), ones_bd) * v_
             for r_, kt, v_, (u, p) in zip(rp, ktp, vp, groups)]

    for j in range(n_sub):
        idx = [i for i, (u, p) in enumerate(groups) if units[u][0] == j]
        bp_ = [(units[groups[i][0]][1], groups[i][1]) for i in idx]
        st = [state_ref[g] for g in bp_]
        u_st = [_dot_nt(wu[i][:, :LANES], s_) + wu[i][:, LANES:] for i, s_ in zip(idx, st)]
        rs = [_dot_nt(rg[i], s_) for i, s_ in zip(idx, st)]
        qu = [_dot_hi(q_m[i], x) for i, x in zip(idx, u_st)]
        ubd = [_dot_tn(x, bd_st[i]) for i, x in zip(idx, u_st)]
        y = [x + _unstack(pv[i] - q_) for i, x, q_ in zip(idx, rs, qu)]
        for n, (i, g) in enumerate(zip(idx, bp_)):
            state_ref[g] = st[n] * gtot[i] + vkd[i] - ubd[n]
        mean = [_head_sum(x, ones_bd) * inv_n for x in y]
        d = [x - mu_ for x, mu_ in zip(y, mean)]
        var = [_head_sum(x * x, ones_bd) * inv_n for x in d]
        for n, (i, (b, p)) in enumerate(zip(idx, bp_)):
            yn = d[n] * lax.rsqrt(var[n] + RWKV_GN_EPS) * tile(lnw_ref, p) + tile(lnb_ref, p)
            o_ref[b, rows(j), L + p * LANES:L + (p + 1) * LANES] = (
                (yn + bonus[i]) * tile(g_out[groups[i][0]], p)).astype(BF16)
    for b in range(n_batch):
        zprev_ref[b] = z_ref[b, n_sub * c - TAIL:, :]


def _pair_blockdiag(w):
    g = w.reshape(N_PAIRS, PAIR, LRU_BLOCK, LRU_BLOCK)
    z = jnp.zeros((N_PAIRS, LRU_BLOCK, LRU_BLOCK), w.dtype)
    top = jnp.concatenate([g[:, 0], z], axis=2)
    bot = jnp.concatenate([z, g[:, 1]], axis=2)
    return jnp.concatenate([top, bot], axis=1)


def _mixer_ab(z, conv_w, conv_b, ga_w, ga_b, gx_w, gx_b, a_param, mu, w0, w2, a0, a2, g2,
              k_k, k_a, r_k, ln_w, ln_b):
    b, s, n = z.shape
    rows = CHUNKS_PER_STEP * CHUNK
    assert s % rows == 0 and n == AB_IN, z.shape
    row = lambda t: t.reshape(1, -1)
    w2p = jnp.concatenate([w2, jnp.zeros((ICLR_RANK, RWKV_WIDTH), F32)], axis=0).astype(BF16)
    a2p = jnp.concatenate([jnp.zeros((DECAY_RANK, RWKV_WIDTH), F32), a2], axis=0).astype(BF16)
    row_params = [conv_b, ga_b, gx_b, a_param, w0, a0, k_k, k_a, r_k, ln_w, ln_b]
    assert len(row_params) == N_ROW_PARAMS
    table = jnp.concatenate([conv_w] + [row(t) for t in row_params]
                            + [jnp.zeros((1, LRU_WIDTH), F32)], axis=0)
    gate_w = jnp.concatenate([_pair_blockdiag(ga_w), _pair_blockdiag(gx_w)], axis=0).astype(BF16)
    low_rank = jnp.stack([w2p, a2p, g2.astype(BF16)])
    args = (z, table, row(mu), gate_w, low_rank)

    def full(a):
        nd = a.ndim
        return pl.BlockSpec(a.shape, lambda i: (0,) * nd)

    return pl.pallas_call(
        _mixer_ab_kernel,
        grid=(s // rows,),
        in_specs=[pl.BlockSpec((b, rows, n), lambda i: (0, i, 0))] + [full(a) for a in args[1:]],
        out_specs=pl.BlockSpec((b, rows, D_MODEL), lambda i: (0, i, 0)),
        out_shape=jax.ShapeDtypeStruct((b, s, D_MODEL), BF16),
        scratch_shapes=[
            pltpu.VMEM((b, TAIL, n), F32),
            pltpu.VMEM((b, TAIL, LRU_WIDTH), F32),
            pltpu.VMEM((b, N_PAIRS, LANES, LANES), F32),
        ],
        compiler_params=pltpu.CompilerParams(
            dimension_semantics=("arbitrary",), vmem_limit_bytes=VMEM_LIMIT),
        name="mixer_ab",
    )(*args)


def _mixer_c_kernel(z_ref, cw_ref, alog_ref, dt_ref, nw_ref, o_ref, zprev_ref, state_ref):
    n_batch = z_ref.shape[0]
    c = CHUNK
    W = DN_WIDTH
    dh = DN_HEAD_DIM

    @pl.when(pl.program_id(0) == 0)
    def _():
        zprev_ref[...] = jnp.zeros_like(zprev_ref)
        state_ref[...] = jnp.zeros_like(state_ref)

    ri = _iota((c, c), 0)
    ci = _iota((c, c), 1)
    causal = ri >= ci
    strict = ri > ci
    l_incl = jnp.where(causal, 1.0, 0.0)

    n_sub = z_ref.shape[1] // c
    units = [(j, b) for j in range(n_sub) for b in range(n_batch)]
    groups = [(u, h) for u in range(len(units)) for h in range(DN_HEADS)]
    rows = lambda j: slice(j * c, (j + 1) * c)
    zq = [z_ref[b, rows(j), :3 * W] for j, b in units]
    tail = [zprev_ref[b] if j == 0 else z_ref[b, j * c - TAIL:j * c, :3 * W] for j, b in units]
    qkv = [_silu_of_half(_causal_conv(x, t, cw_ref, 0.5)) for x, t in zip(zq, tail)]
    zs = [z_ref[b, rows(j), 4 * W:] for j, b in units]
    beta_all = [_sigmoid(x) for x in zs]
    g_all = [-jnp.exp(alog_ref[...]) * _softplus(x + dt_ref[...]) for x in zs]
    gc_all = [_cumsum_rows(l_incl, x) for x in g_all]
    gc_rows = [x.T for x in gc_all]

    beta = [beta_all[u][:, h:h + 1] for u, h in groups]
    gcol = [gc_all[u][:, DN_HEADS + h:DN_HEADS + h + 1] for u, h in groups]
    grow = [gc_rows[u][DN_HEADS + h:DN_HEADS + h + 1, :] for u, h in groups]
    glast = [x[c - 1:c, :] for x in gcol]
    q = [qkv[u][:, h * dh:(h + 1) * dh] for u, h in groups]
    k = [qkv[u][:, W + h * dh:W + (h + 1) * dh] for u, h in groups]
    v = [qkv[u][:, 2 * W + h * dh:2 * W + (h + 1) * dh] for u, h in groups]
    ones = jnp.ones((dh, dh), BF16)
    lane_sum = lambda x: jnp.dot(x.astype(BF16), ones, preferred_element_type=F32)
    q = [x * (lax.rsqrt(lane_sum(x * x) + 1e-6) * (dh ** -0.5)) for x in q]
    k = [x * lax.rsqrt(lane_sum(x * x) + 1e-6) for x in k]
    decay = [jnp.exp(jnp.where(causal, gc - gr, -jnp.inf)) for gc, gr in zip(gcol, grow)]
    kk = [_dot_nt(x, x) for x in k]
    a_m = [jnp.where(strict, bt * m * d, 0.0) for bt, m, d in zip(beta, kk, decay)]
    t_inv = _unit_lower_inverse_many(a_m)
    egc = [jnp.exp(x) for x in gcol]
    rhs = [jnp.concatenate([vv * bt, kx * (bt * e)], axis=1)
           for vv, kx, bt, e in zip(v, k, beta, egc)]
    sol = [_dot_hi(t, r) for t, r in zip(t_inv, rhs)]
    attn = [_dot_nt(qx, kx) * d for qx, kx, d in zip(q, k, decay)]
    qe = [qx * e for qx, e in zip(q, egc)]
    kdec = [kx * jnp.exp(gl - gc) for kx, gl, gc in zip(k, glast, gcol)]
    eg = [jnp.exp(x) for x in glast]

    for j in range(n_sub):
        idx = [i for i, (u, h) in enumerate(groups) if units[u][0] == j]
        bh = [(units[groups[i][0]][1], groups[i][1]) for i in idx]
        st = [state_ref[g] for g in bh]
        ws = [_dot_hi(sol[i][:, dh:], s_) for i, s_ in zip(idx, st)]
        v_new = [sol[i][:, :dh] - y for i, y in zip(idx, ws)]
        qs = [_dot_hi(qe[i], s_) for i, s_ in zip(idx, st)]
        av = [_dot_hi(attn[i], vn) for i, vn in zip(idx, v_new)]
        kv = [_dot_tn(kdec[i], vn) for i, vn in zip(idx, v_new)]
        o = [x + y for x, y in zip(qs, av)]
        ms = [lane_sum(x * x) for x in o]
        for n, (i, (b, h)) in enumerate(zip(idx, bh)):
            state_ref[b, h] = st[n] * eg[i] + kv[n]
            gate = z_ref[b, rows(j), 3 * W + h * dh:3 * W + (h + 1) * dh]
            o_ref[b, rows(j), h * dh:(h + 1) * dh] = (o[n] * lax.rsqrt(
                ms[n] * (1.0 / dh) + NORM_EPS) * (nw_ref[...] * _silu(gate))).astype(BF16)
    for b in range(n_batch):
        zprev_ref[b] = z_ref[b, n_sub * c - TAIL:, :3 * W]


def _mixer_c(z, conv_w, a_log, dt_bias, norm_w):
    b, s, n = z.shape
    rows = CHUNKS_PER_STEP * CHUNK
    assert s % rows == 0 and n == C_IN_PAD, z.shape
    pad = jnp.zeros((C_SMALL - 2 * DN_HEADS,), F32)
    lead = jnp.zeros((DN_HEADS,), F32)
    alog_row = jnp.concatenate([lead, a_log, pad]).reshape(1, C_SMALL)
    dt_row = jnp.concatenate([lead, dt_bias, pad]).reshape(1, C_SMALL)
    args = (z, conv_w, alog_row, dt_row, norm_w.reshape(1, DN_HEAD_DIM))

    def full(a):
        nd = a.ndim
        return pl.BlockSpec(a.shape, lambda i: (0,) * nd)

    return pl.pallas_call(
        _mixer_c_kernel,
        grid=(s // rows,),
        in_specs=[pl.BlockSpec((b, rows, n), lambda i: (0, i, 0))] + [full(a) for a in args[1:]],
        out_specs=pl.BlockSpec((b, rows, DN_WIDTH), lambda i: (0, i, 0)),
        out_shape=jax.ShapeDtypeStruct((b, s, DN_WIDTH), BF16),
        scratch_shapes=[
            pltpu.VMEM((b, TAIL, 3 * DN_WIDTH), F32),
            pltpu.VMEM((b, DN_HEADS, DN_HEAD_DIM, DN_HEAD_DIM), F32),
        ],
        compiler_params=pltpu.CompilerParams(
            dimension_semantics=("arbitrary",), vmem_limit_bytes=VMEM_LIMIT),
        name="mixer_c",
    )(*args)


def kernel(x, ffn_norm, ffn_w_gate, ffn_w_up, ffn_w_down, mix_norm, final_norm, ab_w_in, lru_conv_w, lru_conv_b, lru_gate_a_w, lru_gate_a_b, lru_gate_x_w, lru_gate_x_b, lru_a_param, rwkv_mu, rwkv_w0, rwkv_w2, rwkv_a0, rwkv_a2, rwkv_g2, rwkv_k_k, rwkv_k_a, rwkv_r_k, rwkv_ln_w, rwkv_ln_b, ab_w_out, c_w_in, dn_conv_w, dn_A_log, dn_dt_bias, dn_norm, c_w_out):
    b, s, d = x.shape
    depth = ffn_norm.shape[0]
    ffn_w = (ffn_w_gate, ffn_w_up, ffn_w_down)
    xf = x.reshape(b * s, d)
    for l in range(depth):
        i = l // 2
        xf = _ffn(xf, ffn_norm[l, 0], *ffn_w, (l, 0), final_norm, final_norm=False)
        if l % 2 == 0:
            z = _norm_proj(xf, mix_norm[l], ab_w_in, i).reshape(b, s, AB_IN)
            y = _mixer_ab(z, lru_conv_w[i], lru_conv_b[i], lru_gate_a_w[i], lru_gate_a_b[i],
                          lru_gate_x_w[i], lru_gate_x_b[i], lru_a_param[i], rwkv_mu[i],
                          rwkv_w0[i], rwkv_w2[i], rwkv_a0[i], rwkv_a2[i], rwkv_g2[i],
                          rwkv_k_k[i], rwkv_k_a[i], rwkv_r_k[i], rwkv_ln_w[i], rwkv_ln_b[i])
            proj = (y.reshape(b * s, D_MODEL), ab_w_out, i)
        else:
            z = _norm_proj_t(xf, mix_norm[l], jnp.swapaxes(c_w_in, 1, 2), i).reshape(b, s, C_IN_PAD)
            y = _mixer_c(z, dn_conv_w[i], dn_A_log[i], dn_dt_bias[i], dn_norm[i])
            proj = (y.reshape(b * s, DN_WIDTH), c_w_out, i)
        xf = _ffn(xf, ffn_norm[l, 1], *ffn_w, (l, 1), final_norm,
                  final_norm=(l == depth - 1), proj=proj)
    return xf.reshape(b, s, d)
```
